```python
import jax
import jax.numpy as jnp
from jax import lax
import numpy as np

D_MODEL = 2048
BATCH = 2
SEQ = 8192
DEPTH = 2

LRU_WIDTH = D_MODEL // 2
LRU_HEADS = 4
LRU_BLOCK = LRU_WIDTH // LRU_HEADS
CONV_WIDTH = 4
LRU_C = 8.0
HGRN_WIDTH = D_MODEL // 2
HGRN_EXPAND = 128
HGRN_HEADS = HGRN_WIDTH // HGRN_EXPAND
HGRN_HEAD_V = HGRN_WIDTH // HGRN_HEADS
HGRN_CHUNK = 64
ATTN_HEADS = 16
HEAD_DIM = D_MODEL // ATTN_HEADS
ROPE_DIM = HEAD_DIM // 4
ROPE_THETA = 500000.0
DILATED_GROUPS = ((128, 1), (512, 4), (2048, 16))
DSWA_BLOCK = 128
D_FF = 4 * D_MODEL
NORM_EPS = 1e-6
N_EVEN = (DEPTH + 1) // 2
N_ODD = DEPTH // 2
IN_SPLITS = (LRU_WIDTH, 2 * LRU_WIDTH, 2 * LRU_WIDTH + HGRN_WIDTH,
             2 * LRU_WIDTH + 2 * HGRN_WIDTH, 2 * LRU_WIDTH + 3 * HGRN_WIDTH)
IN_COLS = 2 * LRU_WIDTH + 4 * HGRN_WIDTH

kernel_name = 'hybrid_rglru_hgrn2_dilated_swa'

F32 = jnp.float32


def rms_norm(x, gain):
    xf = x.astype(F32)
    y = xf * lax.rsqrt(jnp.mean(xf * xf, axis=-1, keepdims=True) + NORM_EPS)
    return (y * gain.astype(F32)).astype(x.dtype)


def causal_depthwise_conv(x, w, b):
    out = lax.conv_general_dilated(
        x, w[:, None, :].astype(x.dtype), window_strides=(1,),
        padding=[(CONV_WIDTH - 1, 0)], dimension_numbers=('NWC', 'WIO', 'NWC'),
        feature_group_count=x.shape[-1])
    return out + b.astype(x.dtype)


def rg_lru(xb, w_a, b_a, w_i, b_i, lam):
    bsz, seq, _ = xb.shape
    xf = xb.astype(F32)
    xh = xf.reshape(bsz, seq, LRU_HEADS, LRU_BLOCK)
    r = jax.nn.sigmoid(jnp.einsum('bshi,hij->bshj', xh, w_a.astype(F32)).reshape(bsz, seq, LRU_WIDTH)
                       + b_a.astype(F32))
    i = jax.nn.sigmoid(jnp.einsum('bshi,hij->bshj', xh, w_i.astype(F32)).reshape(bsz, seq, LRU_WIDTH)
                       + b_i.astype(F32))
    log_a = -LRU_C * r * jax.nn.softplus(-lam.astype(F32))
    a = jnp.exp(log_a)
    u = jnp.sqrt(-jnp.expm1(2.0 * log_a)) * (i * xf)

    def combine(left, right):
        a_l, h_l = left
        a_r, h_r = right
        return a_l * a_r, a_r * h_l + h_r

    _, h = lax.associative_scan(combine, (a, u), axis=1)
    return h


def hgrn2(q_raw, f_raw, v_raw, g_raw, lower_bound, g_norm):
    bsz, seq, _ = q_raw.shape
    n_chunks = seq // HGRN_CHUNK
    q = jax.nn.silu(q_raw.astype(F32))
    fz = f_raw.astype(F32)
    log_f = jnp.log(lower_bound + (1.0 - lower_bound) * jax.nn.sigmoid(fz))
    k = (1.0 - lower_bound) * jax.nn.sigmoid(-fz)

    def to_chunks(t, d):
        return t.reshape(bsz, n_chunks, HGRN_CHUNK, HGRN_HEADS, d).transpose(1, 0, 3, 2, 4)

    xs = (to_chunks(q, HGRN_EXPAND), to_chunks(k, HGRN_EXPAND),
          to_chunks(log_f, HGRN_EXPAND), to_chunks(v_raw.astype(F32), HGRN_HEAD_V))
    causal = jnp.tril(jnp.ones((HGRN_CHUNK, HGRN_CHUNK), dtype=bool))[None, None, :, :, None]

    def chunk_step(state, inp):
        qc, kc, gc, vc = inp
        b = jnp.cumsum(gc, axis=2)
        b_last = b[:, :, -1:, :]
        o_inter = jnp.einsum('bhtk,bhkv->bhtv', qc * jnp.exp(b), state)
        decay = jnp.exp(jnp.where(causal, b[:, :, :, None, :] - b[:, :, None, :, :], -jnp.inf))
        scores = jnp.einsum('bhtk,bhsk,bhtsk->bhts', qc, kc, decay)
        o_intra = jnp.einsum('bhts,bhsv->bhtv', scores, vc)
        new_state = (jnp.exp(b_last[:, :, 0, :])[..., None] * state
                     + jnp.einsum('bhsk,bhsv->bhkv', kc * jnp.exp(b_last - b), vc))
        return new_state, o_inter + o_intra

    state0 = jnp.zeros((bsz, HGRN_HEADS, HGRN_EXPAND, HGRN_HEAD_V), F32)
    _, o = lax.scan(chunk_step, state0, xs)
    o = o.transpose(1, 0, 3, 2, 4).reshape(bsz, seq, HGRN_HEADS, HGRN_HEAD_V)
    o = o * lax.rsqrt(jnp.mean(o * o, axis=-1, keepdims=True) + NORM_EPS)
    o = o.reshape(bsz, seq, HGRN_WIDTH) * g_norm.astype(F32)
    return o * jax.nn.silu(g_raw.astype(F32))


def recurrent_mixers(h, w_in, conv_w, conv_b, w_a, b_a, w_i, b_i, lam, lower_bound, g_norm, w_out):
    proj = h @ w_in
    x_lru, y_lru, q_h, f_h, v_h, g_h = jnp.split(proj, list(IN_SPLITS), axis=-1)
    lru = rg_lru(causal_depthwise_conv(x_lru, conv_w, conv_b), w_a, b_a, w_i, b_i, lam)
    lru = lru * jax.nn.gelu(y_lru.astype(F32), approximate=True)
    hg = hgrn2(q_h, f_h, v_h, g_h, lower_bound, g_norm)
    mixed = jnp.concatenate([lru, hg], axis=-1).astype(h.dtype)
    return mixed @ w_out


def partial_rope(t, positions):
    half = ROPE_DIM // 2
    inv_freq = 1.0 / (ROPE_THETA ** (jnp.arange(half, dtype=F32) * (2.0 / ROPE_DIM)))
    ang = positions.astype(F32)[:, :, None, None] * inv_freq
    cos, sin = jnp.cos(ang), jnp.sin(ang)
    t1 = t[..., :half]
    t2 = t[..., half:ROPE_DIM]
    return jnp.concatenate([t1 * cos - t2 * sin, t2 * cos + t1 * sin, t[..., ROPE_DIM:]], axis=-1)


def dilated_branch(q, k, v, window, dilation):
    bsz, seq, nh, hd = q.shape
    n_dist = window // dilation
    qb = DSWA_BLOCK
    length = seq // dilation
    nb = -(-length // qb)
    lp = nb * qb
    groups = bsz * dilation

    def to_sub(t):
        t = t.reshape(bsz, length, dilation, nh, hd).transpose(0, 2, 3, 1, 4).reshape(groups, nh, length, hd)
        t = jnp.pad(t, ((0, 0), (0, 0), (0, lp - length), (0, 0)))
        return t.reshape(groups, nh, nb, qb, hd)

    def band(t):
        prev = jnp.pad(t, ((0, 0), (0, 0), (1, 0), (0, 0), (0, 0)))[:, :, :-1]
        return jnp.concatenate([prev, t], axis=3)

    def from_sub(t):
        t = t.reshape(groups, nh, lp, -1)[:, :, :length]
        return t.reshape(bsz, dilation, nh, length, -1).transpose(0, 3, 1, 2, 4).reshape(bsz, seq, nh, -1)

    qs = to_sub(q)
    kb = band(to_sub(k))
    vb = band(to_sub(v))
    s = jnp.einsum('ghnqd,ghnkd->ghnqk', qs, kb)
    qi = jnp.arange(qb)[:, None]
    kj = jnp.arange(2 * qb)[None, :]
    dist = qi - kj + qb
    key_idx = jnp.arange(nb)[:, None, None] * qb - qb + kj[None]
    mask = (dist >= 0) & (dist <= n_dist) & (key_idx >= 0)
    s = jnp.where(mask, s, -jnp.inf)
    m = jnp.max(s, axis=-1)
    p = jnp.exp(s - m[..., None])
    l = jnp.sum(p, axis=-1)
    o = jnp.einsum('ghnqk,ghnkd->ghnqd', p, vb)
    return from_sub(m), from_sub(l), from_sub(o)


def dilated_attention(h, positions, w_qkv, w_o):
    bsz, seq, _ = h.shape
    qkv = (h @ w_qkv).astype(F32).reshape(bsz, seq, 3, ATTN_HEADS, HEAD_DIM)
    q = partial_rope(qkv[:, :, 0], positions) * (HEAD_DIM ** -0.5)
    k = partial_rope(qkv[:, :, 1], positions)
    v = qkv[:, :, 2]
    branches = [dilated_branch(q, k, v, w, d) for (w, d) in DILATED_GROUPS]
    m_max = branches[0][0]
    for br in branches[1:]:
        m_max = jnp.maximum(m_max, br[0])
    w0 = jnp.exp(branches[0][0] - m_max)
    num = w0 * branches[0][2]
    den = w0 * branches[0][1]
    for m_b, l_b, o_b in branches[1:]:
        w_b = jnp.exp(m_b - m_max)
        num = num + w_b * o_b
        den = den + w_b * l_b
    out = (num / den).reshape(bsz, seq, D_MODEL).astype(h.dtype)
    return out @ w_o


def squared_relu_mlp(h, w1, w2):
    return jnp.square(jax.nn.relu(h @ w1)) @ w2


def setup_inputs(seed: int = 0) -> dict:
    key = jax.random.key(seed)
    ks = jax.random.split(key, 20)

    def nrm(k, shape, scale):
        return jax.random.normal(k, shape, F32) * scale

    x = nrm(ks[0], (BATCH, SEQ, D_MODEL), 1.0)
    positions = jnp.broadcast_to(jnp.arange(SEQ, dtype=jnp.int32), (BATCH, SEQ))
    norm_mix = 1.0 + nrm(ks[1], (DEPTH, D_MODEL), 0.05)
    norm_mlp = 1.0 + nrm(ks[2], (DEPTH, D_MODEL), 0.05)
    final_norm = 1.0 + nrm(ks[3], (D_MODEL,), 0.05)
    rec_w_in = nrm(ks[4], (N_EVEN, D_MODEL, IN_COLS), D_MODEL ** -0.5)
    rec_conv_w = nrm(ks[5], (N_EVEN, CONV_WIDTH, LRU_WIDTH), CONV_WIDTH ** -0.5)
    rec_conv_b = nrm(ks[6], (N_EVEN, LRU_WIDTH), 0.01)
    lru_w_a = nrm(ks[7], (N_EVEN, LRU_HEADS, LRU_BLOCK, LRU_BLOCK), LRU_BLOCK ** -0.5)
    lru_b_a = nrm(ks[8], (N_EVEN, LRU_WIDTH), 0.01)
    lru_w_i = nrm(ks[9], (N_EVEN, LRU_HEADS, LRU_BLOCK, LRU_BLOCK), LRU_BLOCK ** -0.5)
    lru_b_i = nrm(ks[10], (N_EVEN, LRU_WIDTH), 0.01)
    a_pow_c = jax.random.uniform(ks[11], (N_EVEN, LRU_WIDTH), F32, minval=0.9, maxval=0.999)
    a_base = a_pow_c ** (1.0 / LRU_C)
    lru_lambda = jnp.log(a_base) - jnp.log1p(-a_base)
    hgrn_lb_logits = nrm(ks[12], (DEPTH + 1, HGRN_WIDTH), 0.5)
    hgrn_g_norm = 1.0 + nrm(ks[13], (N_EVEN, HGRN_WIDTH), 0.05)
    rec_w_out = nrm(ks[14], (N_EVEN, D_MODEL, D_MODEL), D_MODEL ** -0.5)
    attn_w_qkv = nrm(ks[15], (N_ODD, D_MODEL, 3 * D_MODEL), D_MODEL ** -0.5)
    attn_w_o = nrm(ks[16], (N_ODD, D_MODEL, D_MODEL), D_MODEL ** -0.5)
    mlp_w1 = nrm(ks[17], (DEPTH, D_MODEL, D_FF), D_MODEL ** -0.5)
    mlp_w2 = nrm(ks[18], (DEPTH, D_FF, D_MODEL), D_FF ** -0.5)
    return {'x': x, 'positions': positions, 'norm_mix': norm_mix, 'norm_mlp': norm_mlp,
            'final_norm': final_norm, 'rec_w_in': rec_w_in, 'rec_conv_w': rec_conv_w,
            'rec_conv_b': rec_conv_b, 'lru_w_a': lru_w_a, 'lru_b_a': lru_b_a,
            'lru_w_i': lru_w_i, 'lru_b_i': lru_b_i, 'lru_lambda': lru_lambda,
            'hgrn_lb_logits': hgrn_lb_logits, 'hgrn_g_norm': hgrn_g_norm,
            'rec_w_out': rec_w_out, 'attn_w_qkv': attn_w_qkv, 'attn_w_o': attn_w_o,
            'mlp_w1': mlp_w1, 'mlp_w2': mlp_w2}


def reference(x, positions, norm_mix, norm_mlp, final_norm, rec_w_in, rec_conv_w, rec_conv_b,
              lru_w_a, lru_b_a, lru_w_i, lru_b_i, lru_lambda, hgrn_lb_logits, hgrn_g_norm,
              rec_w_out, attn_w_qkv, attn_w_o, mlp_w1, mlp_w2):
    lower_bounds = jnp.cumsum(jax.nn.softmax(hgrn_lb_logits.astype(F32), axis=0), axis=0)
    h = x
    for layer in range(DEPTH):
        hn = rms_norm(h, norm_mix[layer])
        j = layer // 2
        if layer % 2 == 0:
            mix = recurrent_mixers(hn, rec_w_in[j], rec_conv_w[j], rec_conv_b[j], lru_w_a[j],
                                   lru_b_a[j], lru_w_i[j], lru_b_i[j], lru_lambda[j],
                                   lower_bounds[layer], hgrn_g_norm[j], rec_w_out[j])
        else:
            mix = dilated_attention(hn, positions, attn_w_qkv[j], attn_w_o[j])
        h = h + mix
        h = h + squared_relu_mlp(rms_norm(h, norm_mlp[layer]), mlp_w1[layer], mlp_w2[layer])
    return rms_norm(h, final_norm)
```

```python
import functools
import math

import jax
import jax.numpy as jnp
from jax import lax
from jax.experimental import pallas as pl
from jax.experimental.pallas import tpu as pltpu

F32 = jnp.float32
BF16 = jnp.bfloat16

NORM_EPS = 1e-6
LRU_C = 8.0
LRU_HEADS = 4
CONV_WIDTH = 4
HGRN_HEAD = 128
ATTN_HEADS = 16
HEAD_DIM = 128
ROPE_DIM = 32
ROPE_THETA = 500000.0
DILATIONS = (1, 4, 16)
ATTN_BLOCK = 128
LANES = 128
CONV_HALO = 8
NEG_BIG = -1e30

VMEM_LIMIT = 56 * 1024 * 1024


def _params(*sem):
    return pltpu.CompilerParams(dimension_semantics=sem, vmem_limit_bytes=VMEM_LIMIT)


def _rms_norm(x, gain):
    return x * lax.rsqrt(jnp.mean(x * x, axis=-1, keepdims=True) + NORM_EPS) * gain


def _dot(a, b):
    return jnp.dot(a, b, preferred_element_type=F32)


def _dot_nt(a, b):
    return lax.dot_general(a, b, (((1,), (1,)), ((), ())), preferred_element_type=F32)


def _dot_tn(a, b):
    return lax.dot_general(a, b, (((0,), (0,)), ((), ())), preferred_element_type=F32)


def _norm_matmul_kernel(x_ref, g_ref, w_ref, o_ref, xn_ref):
    @pl.when(pl.program_id(1) == 0)
    def _():
        xn_ref[...] = _rms_norm(x_ref[...], g_ref[...]).astype(BF16)

    o_ref[...] = _dot(xn_ref[...], w_ref[...])


def _norm_matmul(x, gain, w, *, tm=1024, tn=512):
    m, d = x.shape
    n = w.shape[1]
    return pl.pallas_call(
        _norm_matmul_kernel,
        grid=(m // tm, n // tn),
        in_specs=[pl.BlockSpec((tm, d), lambda i, j: (i, 0)),
                  pl.BlockSpec((1, d), lambda i, j: (0, 0)),
                  pl.BlockSpec((d, tn), lambda i, j: (0, j))],
        out_specs=pl.BlockSpec((tm, tn), lambda i, j: (i, j)),
        out_shape=jax.ShapeDtypeStruct((m, n), F32),
        scratch_shapes=[pltpu.VMEM((tm, d), BF16)],
        compiler_params=_params("parallel", "arbitrary"),
        name="norm_matmul",
    )(x, gain.reshape(1, d), w)


def _qkv_kernel(x_ref, g_ref, w_ref, pos_ref, freq_ref, sign_ref, o_ref, xn_ref, cos_ref, sin_ref,
                *, q_tiles, qk_tiles, heads_per_tile):
    j = pl.program_id(1)

    @pl.when(j == 0)
    def _():
        xn_ref[...] = _rms_norm(x_ref[...], g_ref[...]).astype(BF16)
        ang = pos_ref[...] * freq_ref[...]
        cos_ref[...] = jnp.cos(ang)
        sin_ref[...] = jnp.sin(ang) * sign_ref[...]

    acc = _dot(xn_ref[...], w_ref[...])

    @pl.when(j < qk_tiles)
    def _():
        scale = jnp.where(j < q_tiles, F32(HEAD_DIM ** -0.5), F32(1.0))
        cos = cos_ref[...]
        sin = sin_ref[...]
        lane = lax.broadcasted_iota(jnp.int32, cos.shape, 1)
        half = ROPE_DIM // 2
        for hh in range(heads_per_tile):
            t = acc[:, hh * HEAD_DIM:(hh + 1) * HEAD_DIM]
            partner = jnp.where(lane < half, pltpu.roll(t, HEAD_DIM - half, axis=1),
                                pltpu.roll(t, half, axis=1))
            o_ref[:, hh * HEAD_DIM:(hh + 1) * HEAD_DIM] = (t * cos + partner * sin) * scale

    @pl.when(j >= qk_tiles)
    def _():
        o_ref[...] = acc


def _qkv_proj(x, gain, w, pos, *, tm=1024, tn=512):
    m, d = x.shape
    n = w.shape[1]
    half = ROPE_DIM // 2
    inv_freq = 1.0 / (ROPE_THETA ** (jnp.arange(half, dtype=F32) * (2.0 / ROPE_DIM)))
    zeros = jnp.zeros((HEAD_DIM - ROPE_DIM,), F32)
    freq = jnp.concatenate([inv_freq, inv_freq, zeros]).reshape(1, HEAD_DIM)
    sign = jnp.concatenate([-jnp.ones((half,), F32), jnp.ones((half,), F32), zeros]).reshape(1, HEAD_DIM)
    d_attn = n // 3
    kern = functools.partial(_qkv_kernel, q_tiles=d_attn // tn, qk_tiles=2 * d_attn // tn,
                             heads_per_tile=tn // HEAD_DIM)
    return pl.pallas_call(
        kern,
        grid=(m // tm, n // tn),
        in_specs=[pl.BlockSpec((tm, d), lambda i, j: (i, 0)),
                  pl.BlockSpec((1, d), lambda i, j: (0, 0)),
                  pl.BlockSpec((d, tn), lambda i, j: (0, j)),
                  pl.BlockSpec((tm, 1), lambda i, j: (i, 0)),
                  pl.BlockSpec((1, HEAD_DIM), lambda i, j: (0, 0)),
                  pl.BlockSpec((1, HEAD_DIM), lambda i, j: (0, 0))],
        out_specs=pl.BlockSpec((tm, tn), lambda i, j: (i, j)),
        out_shape=jax.ShapeDtypeStruct((m, n), F32),
        scratch_shapes=[pltpu.VMEM((tm, d), BF16), pltpu.VMEM((tm, HEAD_DIM), F32),
                        pltpu.VMEM((tm, HEAD_DIM), F32)],
        compiler_params=_params("parallel", "arbitrary"),
        name="qkv_rope",
    )(x, gain.reshape(1, d), w, pos.astype(F32).reshape(m, 1), freq, sign)


def _out_proj_kernel(*refs, n_parts):
    h_ref = refs[0]
    a_refs = refs[1:1 + n_parts]
    w_refs = refs[1 + n_parts:1 + 2 * n_parts]
    o_ref = refs[1 + 2 * n_parts]
    acc = h_ref[...]
    for a_ref, w_ref in zip(a_refs, w_refs):
        acc = acc + _dot(a_ref[...], w_ref[...])
    o_ref[...] = acc


def _out_proj(h, parts, weights, *, tm=1024, tn=512):
    m, n = h.shape
    in_specs = [pl.BlockSpec((tm, tn), lambda i, j: (i, j))]
    in_specs += [pl.BlockSpec((tm, a.shape[1]), lambda i, j: (i, 0)) for a in parts]
    in_specs += [pl.BlockSpec((w.shape[0], tn), lambda i, j: (0, j)) for w in weights]
    return pl.pallas_call(
        functools.partial(_out_proj_kernel, n_parts=len(parts)),
        grid=(m // tm, n // tn),
        in_specs=in_specs,
        out_specs=pl.BlockSpec((tm, tn), lambda i, j: (i, j)),
        out_shape=jax.ShapeDtypeStruct((m, n), F32),
        compiler_params=_params("parallel", "arbitrary"),
        name="out_proj",
    )(h, *parts, *weights)


def _mlp_kernel(h_ref, g_ref, w1_ref, w2_ref, fg_ref, o_ref, xn_ref, *, final_norm):
    k = pl.program_id(1)

    @pl.when(k == 0)
    def _():
        x = h_ref[...]
        xn_ref[...] = _rms_norm(x, g_ref[...]).astype(BF16)
        o_ref[...] = x

    a = jnp.maximum(_dot(xn_ref[...], w1_ref[...]), 0.0)
    o_ref[...] += _dot((a * a).astype(BF16), w2_ref[...])

    if final_norm:
        @pl.when(k == pl.num_programs(1) - 1)
        def _():
            o_ref[...] = _rms_norm(o_ref[...], fg_ref[...])


def _mlp(h, gain, w1, w2, final_gain=None, *, tm=512, tf=512):
    m, d = h.shape
    f = w1.shape[1]
    fg = jnp.ones((d,), F32) if final_gain is None else final_gain
    return pl.pallas_call(
        functools.partial(_mlp_kernel, final_norm=final_gain is not None),
        grid=(m // tm, f // tf),
        in_specs=[pl.BlockSpec((tm, d), lambda i, k: (i, 0)),
                  pl.BlockSpec((1, d), lambda i, k: (0, 0)),
                  pl.BlockSpec((d, tf), lambda i, k: (0, k)),
                  pl.BlockSpec((tf, d), lambda i, k: (k, 0)),
                  pl.BlockSpec((1, d), lambda i, k: (0, 0))],
        out_specs=pl.BlockSpec((tm, d), lambda i, k: (i, 0)),
        out_shape=jax.ShapeDtypeStruct((m, d), F32),
        scratch_shapes=[pltpu.VMEM((tm, d), BF16)],
        compiler_params=_params("parallel", "arbitrary"),
        name="mlp",
    )(h, gain.reshape(1, d), w1, w2, fg.reshape(1, d))


def _linear_scan(a, u):
    rows = a.shape[0]
    row = lax.broadcasted_iota(jnp.int32, a.shape, 0)
    d = 1
    while d < rows:
        keep = row >= d
        u = u + a * jnp.where(keep, pltpu.roll(u, d, axis=0), 0.0)
        a = a * jnp.where(keep, pltpu.roll(a, d, axis=0), 1.0)
        d *= 2
    return a, u


def _lru_kernel(x_ref, y_ref, cw_ref, cb_ref, wa_ref, ba_ref, wi_ref, bi_ref, lam_ref, o_ref,
                xbuf_ref, carry_ref, *, tt):
    t = pl.program_id(1)

    @pl.when(t == 0)
    def _():
        xbuf_ref[0:CONV_HALO, :] = jnp.zeros((CONV_HALO, xbuf_ref.shape[1]), F32)
        carry_ref[...] = jnp.zeros(carry_ref.shape, F32)

    @pl.when(t > 0)
    def _():
        xbuf_ref[0:CONV_HALO, :] = xbuf_ref[tt:tt + CONV_HALO, :]

    xbuf_ref[CONV_HALO:, :] = x_ref[...]

    xc = cb_ref[...] + cw_ref[0:1, :] * xbuf_ref[pl.ds(CONV_HALO - CONV_WIDTH + 1, tt), :]
    for j in range(1, CONV_WIDTH):
        xc = xc + cw_ref[j:j + 1, :] * xbuf_ref[pl.ds(CONV_HALO - CONV_WIDTH + 1 + j, tt), :]

    width = xc.shape[1]
    blk = width // LRU_HEADS
    pre_a = []
    pre_i = []
    for hh in range(LRU_HEADS):
        xh = xc[:, hh * blk:(hh + 1) * blk].astype(BF16)
        pre_a.append(_dot(xh, wa_ref[hh]))
        pre_i.append(_dot(xh, wi_ref[hh]))
    r = jax.nn.sigmoid(jnp.concatenate(pre_a, axis=1) + ba_ref[...])
    gi = jax.nn.sigmoid(jnp.concatenate(pre_i, axis=1) + bi_ref[...])

    neg_lam = -lam_ref[...]
    softplus = jnp.maximum(neg_lam, 0.0) + jnp.log1p(jnp.exp(-jnp.abs(neg_lam)))
    log_a = (-LRU_C) * r * softplus
    a = jnp.exp(log_a)
    u = jnp.sqrt(-jnp.tanh(log_a) * (a * a + 1.0)) * (gi * xc)

    a_cum, h = _linear_scan(a, u)
    h = h + a_cum * carry_ref[...]
    carry_ref[...] = h[tt - 1:tt, :]

    y = y_ref[...]
    gelu = 0.5 * y * (1.0 + jnp.tanh(math.sqrt(2.0 / math.pi) * (y + 0.044715 * (y * y * y))))
    o_ref[...] = (h * gelu).astype(o_ref.dtype)


def _lru(proj, conv_w, conv_b, w_a, b_a, w_i, b_i, lam, *, batch, seq, tt=256):
    width = conv_w.shape[1]
    nt = seq // tt
    row = lambda b, t: (b * nt + t, 0)
    vec = pl.BlockSpec((1, width), lambda b, t: (0, 0))
    gate_w = pl.BlockSpec(w_a.shape, lambda b, t: (0, 0, 0))
    return pl.pallas_call(
        functools.partial(_lru_kernel, tt=tt),
        grid=(batch, nt),
        in_specs=[pl.BlockSpec((tt, width), row),
                  pl.BlockSpec((tt, width), lambda b, t: (b * nt + t, 1)),
                  pl.BlockSpec((CONV_WIDTH, width), lambda b, t: (0, 0)), vec,
                  gate_w, vec, gate_w, vec, vec],
        out_specs=pl.BlockSpec((tt, width), row),
        out_shape=jax.ShapeDtypeStruct((batch * seq, width), BF16),
        scratch_shapes=[pltpu.VMEM((tt + CONV_HALO, width), F32), pltpu.VMEM((1, width), F32)],
        compiler_params=_params("parallel", "arbitrary"),
        name="rg_lru",
    )(proj, proj, conv_w, conv_b.reshape(1, width), w_a, b_a.reshape(1, width),
      w_i, b_i.reshape(1, width), lam.reshape(1, width))


def _cumsum_rows(x):
    rows = x.shape[0]
    row = lax.broadcasted_iota(jnp.int32, x.shape, 0)
    d = 1
    while d < rows:
        x = x + jnp.where(row >= d, pltpu.roll(x, d, axis=0), 0.0)
        d *= 2
    return x


def _hgrn_kernel(q_ref, f_ref, v_ref, g_ref, lbl_ref, gn_ref, o_ref, state_ref, *, layer, chunk, n_chunks):
    @pl.when(pl.program_id(2) == 0)
    def _():
        state_ref[...] = jnp.zeros(state_ref.shape, F32)

    logits = lbl_ref[...]
    e = jnp.exp(logits - jnp.max(logits, axis=0, keepdims=True))
    lb = jnp.sum(e[0:layer + 1, :], axis=0, keepdims=True) / jnp.sum(e, axis=0, keepdims=True)

    row = lax.broadcasted_iota(jnp.int32, (chunk, HGRN_HEAD), 0)
    ti = lax.broadcasted_iota(jnp.int32, (chunk, chunk), 0)
    si = lax.broadcasted_iota(jnp.int32, (chunk, chunk), 1)
    tx = ti ^ si

    for c in range(n_chunks):
        rows = pl.ds(c * chunk, chunk)
        qr = q_ref[rows, :]
        fz = f_ref[rows, :]
        v = v_ref[rows, :].astype(BF16)
        q = qr * jax.nn.sigmoid(qr)
        log_f = jnp.log(lb + (1.0 - lb) * jax.nn.sigmoid(fz))
        kk = (1.0 - lb) * jax.nn.sigmoid(-fz)
        b = _cumsum_rows(log_f)

        scores = jnp.where(tx == 0, jnp.sum(q * kk, axis=-1, keepdims=True), 0.0)
        b_end = b
        s = 1
        while s < chunk:
            upper = (row & s) != 0
            e_q = jnp.where(upper, b - pltpu.roll(b_end, s, axis=0), NEG_BIG)
            e_k = jnp.where(upper, NEG_BIG, b_end - b)
            qd = (q * jnp.exp(e_q)).astype(BF16)
            kd = (kk * jnp.exp(e_k)).astype(BF16)
            scores = scores + jnp.where(tx < 2 * s, _dot_nt(qd, kd), 0.0)
            b_end = jnp.where(upper, b_end, pltpu.roll(b_end, chunk - s, axis=0))
            s *= 2

        state = state_ref[...]
        o = _dot(scores.astype(BF16), v) + _dot_nt((q * jnp.exp(b)).astype(BF16), state.astype(BF16))
        kd = (kk * jnp.exp(b_end - b)).astype(BF16)
        state_ref[...] = state * jnp.exp(b_end[0:1, :]) + _dot_tn(v, kd)

        o = o * lax.rsqrt(jnp.mean(o * o, axis=-1, keepdims=True) + NORM_EPS) * gn_ref[...]
        gr = g_ref[rows, :]
        o_ref[rows, :] = (o * (gr * jax.nn.sigmoid(gr))).astype(o_ref.dtype)


def _hgrn(proj, lb_logits, g_norm, *, batch, seq, layer, col0, tt=512, chunk=128):
    width = g_norm.shape[0]
    heads = width // HGRN_HEAD
    nt = seq // tt
    hb = width // HGRN_HEAD
    c0 = col0 // HGRN_HEAD

    def col(group):
        return pl.BlockSpec((tt, HGRN_HEAD), lambda b, h, t: (b * nt + t, c0 + group * hb + h))

    return pl.pallas_call(
        functools.partial(_hgrn_kernel, layer=layer, chunk=chunk, n_chunks=tt // chunk),
        grid=(batch, heads, nt),
        in_specs=[col(0), col(1), col(2), col(3),
                  pl.BlockSpec((lb_logits.shape[0], HGRN_HEAD), lambda b, h, t: (0, h)),
                  pl.BlockSpec((1, HGRN_HEAD), lambda b, h, t: (0, h))],
        out_specs=pl.BlockSpec((tt, HGRN_HEAD), lambda b, h, t: (b * nt + t, h)),
        out_shape=jax.ShapeDtypeStruct((batch * seq, width), BF16),
        scratch_shapes=[pltpu.VMEM((HGRN_HEAD, HGRN_HEAD), F32)],
        compiler_params=_params("parallel", "parallel", "arbitrary"),
        name="hgrn2",
    )(proj, proj, proj, proj, lb_logits, g_norm.reshape(1, width))


def _attn_kernel(q_ref, k_ref, v_ref, o_ref, kbuf_ref, vbuf_ref, num_ref, max_ref, den_ref, *, tq):
    t = pl.program_id(2)

    @pl.when(t == 0)
    def _():
        kbuf_ref[0:tq, :] = jnp.zeros((tq, HEAD_DIM), F32)
        vbuf_ref[0:tq, :] = jnp.zeros((tq, HEAD_DIM), F32)

    @pl.when(t > 0)
    def _():
        kbuf_ref[0:tq, :] = kbuf_ref[tq:2 * tq, :]
        vbuf_ref[0:tq, :] = vbuf_ref[tq:2 * tq, :]

    kbuf_ref[tq:2 * tq, :] = k_ref[...]
    vbuf_ref[tq:2 * tq, :] = v_ref[...]

    blk = ATTN_BLOCK
    qi = lax.broadcasted_iota(jnp.int32, (blk, 2 * blk), 0)
    kj = lax.broadcasted_iota(jnp.int32, (blk, 2 * blk), 1)
    band = (kj >= qi) & (kj <= qi + blk)

    for dil in DILATIONS:
        per_res = tq // (blk * dil)

        def body(idx, carry, dil=dil, per_res=per_res):
            r = idx // per_res
            n = idx % per_res
            q0 = n * (blk * dil) + r
            if dil == 1:
                q0 = pl.multiple_of(q0, blk)
                q_rows = pl.ds(q0, blk)
                k_rows = pl.ds(pl.multiple_of(tq + q0 - blk, blk), 2 * blk)
            else:
                q_rows = pl.ds(q0, blk, stride=dil)
                k_rows = pl.ds(tq + q0 - blk * dil, 2 * blk, stride=dil)
            qb = q_ref[q_rows, :].astype(BF16)
            kb = kbuf_ref[k_rows, :].astype(BF16)
            vb = vbuf_ref[k_rows, :].astype(BF16)
            s = _dot_nt(qb, kb)
            first_key = jnp.where((t == 0) & (n == 0), blk, 0)
            s = jnp.where(band & (kj >= first_key), s, -jnp.inf)
            m = jnp.max(s, axis=-1, keepdims=True)
            p = jnp.exp(s - m)
            l = jnp.sum(p, axis=-1, keepdims=True)
            o = _dot(p.astype(BF16), vb)
            if dil == DILATIONS[0]:
                max_ref[q_rows, :] = jnp.broadcast_to(m, (blk, HEAD_DIM))
                den_ref[q_rows, :] = jnp.broadcast_to(l, (blk, HEAD_DIM))
                num_ref[q_rows, :] = o
            else:
                m_old = max_ref[q_rows, :]
                m_new = jnp.maximum(m_old, m)
                w_old = jnp.exp(m_old - m_new)
                w_new = jnp.exp(m - m_new)
                max_ref[q_rows, :] = m_new
                den_ref[q_rows, :] = den_ref[q_rows, :] * w_old + l * w_new
                num_ref[q_rows, :] = num_ref[q_rows, :] * w_old + o * w_new
            return carry

        lax.fori_loop(0, tq // blk, body, 0)

    o_ref[...] = (num_ref[...] / den_ref[...]).astype(o_ref.dtype)


def _attention(qkv, *, batch, seq):
    tq = ATTN_BLOCK * DILATIONS[-1]
    nt = seq // tq
    d_attn = ATTN_HEADS * HEAD_DIM

    def col(group):
        return pl.BlockSpec((tq, HEAD_DIM), lambda b, h, t: (b * nt + t, group * ATTN_HEADS + h))

    acc = pltpu.VMEM((tq, HEAD_DIM), F32)
    return pl.pallas_call(
        functools.partial(_attn_kernel, tq=tq),
        grid=(batch, ATTN_HEADS, nt),
        in_specs=[col(0), col(1), col(2)],
        out_specs=pl.BlockSpec((tq, HEAD_DIM), lambda b, h, t: (b * nt + t, h)),
        out_shape=jax.ShapeDtypeStruct((batch * seq, d_attn), BF16),
        scratch_shapes=[pltpu.VMEM((2 * tq, HEAD_DIM), F32), pltpu.VMEM((2 * tq, HEAD_DIM), F32),
                        acc, acc, acc],
        compiler_params=_params("parallel", "parallel", "arbitrary"),
        name="dilated_attention",
    )(qkv, qkv, qkv)


def kernel(x, positions, norm_mix, norm_mlp, final_norm, rec_w_in, rec_conv_w, rec_conv_b, lru_w_a, lru_b_a, lru_w_i, lru_b_i, lru_lambda, hgrn_lb_logits, hgrn_g_norm, rec_w_out, attn_w_qkv, attn_w_o, mlp_w1, mlp_w2):
    batch, seq, d = x.shape
    depth = norm_mix.shape[0]
    lru_width = rec_conv_w.shape[2]
    assert seq % (ATTN_BLOCK * DILATIONS[-1]) == 0
    h = x.reshape(batch * seq, d)
    pos = positions.reshape(batch * seq)
    for layer in range(depth):
        j = layer // 2
        if layer % 2 == 0:
            proj = _norm_matmul(h, norm_mix[layer], rec_w_in[j].astype(BF16))
            lru = _lru(proj, rec_conv_w[j], rec_conv_b[j], lru_w_a[j].astype(BF16), lru_b_a[j],
                       lru_w_i[j].astype(BF16), lru_b_i[j], lru_lambda[j], batch=batch, seq=seq)
            hg = _hgrn(proj, hgrn_lb_logits, hgrn_g_norm[j], batch=batch, seq=seq, layer=layer,
                       col0=2 * lru_width)
            w_out = rec_w_out[j].astype(BF16)
            h = _out_proj(h, [lru, hg], [w_out[:lru_width], w_out[lru_width:]])
        else:
            qkv = _qkv_proj(h, norm_mix[layer], attn_w_qkv[j].astype(BF16), pos)
            attn = _attention(qkv, batch=batch, seq=seq)
            h = _out_proj(h, [attn], [attn_w_o[j].astype(BF16)])
        last = layer == depth - 1
        h = _mlp(h, norm_mlp[layer], mlp_w1[layer].astype(BF16), mlp_w2[layer].astype(BF16),
                 final_norm if last else None)
    if depth == 0:
        h = _rms_norm(h, final_norm)
    return h.reshape(batch, seq, d)
```

```python
import functools
import math

import jax
import jax.numpy as jnp
from jax import lax
from jax.experimental import pallas as pl
from jax.experimental.pallas import tpu as pltpu

F32 = jnp.float32
BF16 = jnp.bfloat16

NORM_EPS = 1e-6
LRU_C = 8.0
LRU_HEADS = 4
CONV_WIDTH = 4
HGRN_HEAD = 128
ATTN_HEADS = 16
HEAD_DIM = 128
ROPE_DIM = 32
ROPE_THETA = 500000.0
DILATIONS = (1, 4, 16)
ATTN_BLOCK = 128
LANES = 128
CONV_HALO = 8
NEG_BIG = -1e30

VMEM_LIMIT = 56 * 1024 * 1024


def _params(*sem):
    return pltpu.CompilerParams(dimension_semantics=sem, vmem_limit_bytes=VMEM_LIMIT)


def _rms_norm(x, gain):
    return x * lax.rsqrt(jnp.mean(x * x, axis=-1, keepdims=True) + NORM_EPS) * gain


def _dot(a, b):
    return jnp.dot(a, b, preferred_element_type=F32)


def _dot_nt(a, b):
    return lax.dot_general(a, b, (((1,), (1,)), ((), ())), preferred_element_type=F32)


def _dot_tn(a, b):
    return lax.dot_general(a, b, (((0,), (0,)), ((), ())), preferred_element_type=F32)


def _norm_matmul_kernel(x_ref, g_ref, w_ref, o_ref, xn_ref):
    @pl.when(pl.program_id(1) == 0)
    def _():
        xn_ref[...] = _rms_norm(x_ref[...], g_ref[...]).astype(BF16)

    o_ref[...] = _dot(xn_ref[...], w_ref[...])


def _norm_matmul(x, gain, w, *, tm=1024, tn=512):
    m, d = x.shape
    n = w.shape[1]
    return pl.pallas_call(
        _norm_matmul_kernel,
        grid=(m // tm, n // tn),
        in_specs=[pl.BlockSpec((tm, d), lambda i, j: (i, 0)),
                  pl.BlockSpec((1, d), lambda i, j: (0, 0)),
                  pl.BlockSpec((d, tn), lambda i, j: (0, j))],
        out_specs=pl.BlockSpec((tm, tn), lambda i, j: (i, j)),
        out_shape=jax.ShapeDtypeStruct((m, n), F32),
        scratch_shapes=[pltpu.VMEM((tm, d), BF16)],
        compiler_params=_params("parallel", "arbitrary"),
        name="norm_matmul",
    )(x, gain.reshape(1, d), w)


def _qkv_kernel(x_ref, g_ref, w_ref, pos_ref, freq_ref, sign_ref, o_ref, xn_ref, cos_ref, sin_ref,
                *, q_tiles, qk_tiles, heads_per_tile):
    j = pl.program_id(1)

    @pl.when(j == 0)
    def _():
        xn_ref[...] = _rms_norm(x_ref[...], g_ref[...]).astype(BF16)
        ang = pos_ref[...] * freq_ref[...]
        cos_ref[...] = jnp.cos(ang)
        sin_ref[...] = jnp.sin(ang) * sign_ref[...]

    acc = _dot(xn_ref[...], w_ref[...])

    @pl.when(j < qk_tiles)
    def _():
        scale = jnp.where(j < q_tiles, F32(HEAD_DIM ** -0.5), F32(1.0))
        cos = cos_ref[...]
        sin = sin_ref[...]
        lane = lax.broadcasted_iota(jnp.int32, cos.shape, 1)
        half = ROPE_DIM // 2
        for hh in range(heads_per_tile):
            t = acc[:, hh * HEAD_DIM:(hh + 1) * HEAD_DIM]
            partner = jnp.where(lane < half, pltpu.roll(t, HEAD_DIM - half, axis=1),
                                pltpu.roll(t, half, axis=1))
            o_ref[:, hh * HEAD_DIM:(hh + 1) * HEAD_DIM] = (t * cos + partner * sin) * scale

    @pl.when(j >= qk_tiles)
    def _():
        o_ref[...] = acc


def _qkv_proj(x, gain, w, pos, *, tm=1024, tn=512):
    m, d = x.shape
    n = w.shape[1]
    half = ROPE_DIM // 2
    inv_freq = 1.0 / (ROPE_THETA ** (jnp.arange(half, dtype=F32) * (2.0 / ROPE_DIM)))
    zeros = jnp.zeros((HEAD_DIM - ROPE_DIM,), F32)
    freq = jnp.concatenate([inv_freq, inv_freq, zeros]).reshape(1, HEAD_DIM)
    sign = jnp.concatenate([-jnp.ones((half,), F32), jnp.ones((half,), F32), zeros]).reshape(1, HEAD_DIM)
    d_attn = n // 3
    kern = functools.partial(_qkv_kernel, q_tiles=d_attn // tn, qk_tiles=2 * d_attn // tn,
                             heads_per_tile=tn // HEAD_DIM)
    return pl.pallas_call(
        kern,
        grid=(m // tm, n // tn),
        in_specs=[pl.BlockSpec((tm, d), lambda i, j: (i, 0)),
                  pl.BlockSpec((1, d), lambda i, j: (0, 0)),
                  pl.BlockSpec((d, tn), lambda i, j: (0, j)),
                  pl.BlockSpec((tm, 1), lambda i, j: (i, 0)),
                  pl.BlockSpec((1, HEAD_DIM), lambda i, j: (0, 0)),
                  pl.BlockSpec((1, HEAD_DIM), lambda i, j: (0, 0))],
        out_specs=pl.BlockSpec((tm, tn), lambda i, j: (i, j)),
        out_shape=jax.ShapeDtypeStruct((m, n), F32),
        scratch_shapes=[pltpu.VMEM((tm, d), BF16), pltpu.VMEM((tm, HEAD_DIM), F32),
                        pltpu.VMEM((tm, HEAD_DIM), F32)],
        compiler_params=_params("parallel", "arbitrary"),
        name="qkv_rope",
    )(x, gain.reshape(1, d), w, pos.astype(F32).reshape(m, 1), freq, sign)


def _out_proj_kernel(*refs, n_parts):
    h_ref = refs[0]
    a_refs = refs[1:1 + n_parts]
    w_refs = refs[1 + n_parts:1 + 2 * n_parts]
    o_ref = refs[1 + 2 * n_parts]
    acc = h_ref[...]
    for a_ref, w_ref in zip(a_refs, w_refs):
        acc = acc + _dot(a_ref[...], w_ref[...])
    o_ref[...] = acc


def _out_proj(h, parts, weights, *, tm=1024, tn=512):
    m, n = h.shape
    in_specs = [pl.BlockSpec((tm, tn), lambda i, j: (i, j))]
    in_specs += [pl.BlockSpec((tm, a.shape[1]), lambda i, j: (i, 0)) for a in parts]
    in_specs += [pl.BlockSpec((w.shape[0], tn), lambda i, j: (0, j)) for w in weights]
    return pl.pallas_call(
        functools.partial(_out_proj_kernel, n_parts=len(parts)),
        grid=(m // tm, n // tn),
        in_specs=in_specs,
        out_specs=pl.BlockSpec((tm, tn), lambda i, j: (i, j)),
        out_shape=jax.ShapeDtypeStruct((m, n), F32),
        compiler_params=_params("parallel", "arbitrary"),
        name="out_proj",
    )(h, *parts, *weights)


def _mlp_kernel(h_ref, g_ref, w1_ref, w2_ref, fg_ref, o_ref, xn_ref, *, final_norm):
    k = pl.program_id(1)

    @pl.when(k == 0)
    def _():
        x = h_ref[...]
        xn_ref[...] = _rms_norm(x, g_ref[...]).astype(BF16)
        o_ref[...] = x

    a = jnp.maximum(_dot(xn_ref[...], w1_ref[...]), 0.0)
    o_ref[...] += _dot((a * a).astype(BF16), w2_ref[...])

    if final_norm:
        @pl.when(k == pl.num_programs(1) - 1)
        def _():
            o_ref[...] = _rms_norm(o_ref[...], fg_ref[...])


def _mlp(h, gain, w1, w2, final_gain=None, *, tm=512, tf=512):
    m, d = h.shape
    f = w1.shape[1]
    fg = jnp.ones((d,), F32) if final_gain is None else final_gain
    return pl.pallas_call(
        functools.partial(_mlp_kernel, final_norm=final_gain is not None),
        grid=(m // tm, f // tf),
        in_specs=[pl.BlockSpec((tm, d), lambda i, k: (i, 0)),
                  pl.BlockSpec((1, d), lambda i, k: (0, 0)),
                  pl.BlockSpec((d, tf), lambda i, k: (0, k)),
                  pl.BlockSpec((tf, d), lambda i, k: (k, 0)),
                  pl.BlockSpec((1, d), lambda i, k: (0, 0))],
        out_specs=pl.BlockSpec((tm, d), lambda i, k: (i, 0)),
        out_shape=jax.ShapeDtypeStruct((m, d), F32),
        scratch_shapes=[pltpu.VMEM((tm, d), BF16)],
        compiler_params=_params("parallel", "arbitrary"),
        name="mlp",
    )(h, gain.reshape(1, d), w1, w2, fg.reshape(1, d))


def _linear_scan(a, u):
    rows = a.shape[0]
    row = lax.broadcasted_iota(jnp.int32, a.shape, 0)
    d = 1
    while d < rows:
        keep = row >= d
        u = u + a * jnp.where(keep, pltpu.roll(u, d, axis=0), 0.0)
        a = a * jnp.where(keep, pltpu.roll(a, d, axis=0), 1.0)
        d *= 2
    return a, u


def _lru_kernel(x_ref, y_ref, cw_ref, cb_ref, wa_ref, ba_ref, wi_ref, bi_ref, lam_ref, o_ref,
                xbuf_ref, carry_ref, *, tt):
    t = pl.program_id(1)

    @pl.when(t == 0)
    def _():
        xbuf_ref[0:CONV_HALO, :] = jnp.zeros((CONV_HALO, xbuf_ref.shape[1]), F32)
        carry_ref[...] = jnp.zeros(carry_ref.shape, F32)

    @pl.when(t > 0)
    def _():
        xbuf_ref[0:CONV_HALO, :] = xbuf_ref[tt:tt + CONV_HALO, :]

    xbuf_ref[CONV_HALO:, :] = x_ref[...]

    xc = cb_ref[...] + cw_ref[0:1, :] * xbuf_ref[pl.ds(CONV_HALO - CONV_WIDTH + 1, tt), :]
    for j in range(1, CONV_WIDTH):
        xc = xc + cw_ref[j:j + 1, :] * xbuf_ref[pl.ds(CONV_HALO - CONV_WIDTH + 1 + j, tt), :]

    width = xc.shape[1]
    blk = width // LRU_HEADS
    pre_a = []
    pre_i = []
    for hh in range(LRU_HEADS):
        xh = xc[:, hh * blk:(hh + 1) * blk].astype(BF16)
        pre_a.append(_dot(xh, wa_ref[hh]))
        pre_i.append(_dot(xh, wi_ref[hh]))
    r = jax.nn.sigmoid(jnp.concatenate(pre_a, axis=1) + ba_ref[...])
    gi = jax.nn.sigmoid(jnp.concatenate(pre_i, axis=1) + bi_ref[...])

    neg_lam = -lam_ref[...]
    softplus = jnp.maximum(neg_lam, 0.0) + jnp.log1p(jnp.exp(-jnp.abs(neg_lam)))
    log_a = (-LRU_C) * r * softplus
    a = jnp.exp(log_a)
    u = jnp.sqrt(-jnp.tanh(log_a) * (a * a + 1.0)) * (gi * xc)

    a_cum, h = _linear_scan(a, u)
    h = h + a_cum * carry_ref[...]
    carry_ref[...] = h[tt - 1:tt, :]

    y = y_ref[...]
    gelu = 0.5 * y * (1.0 + jnp.tanh(math.sqrt(2.0 / math.pi) * (y + 0.044715 * (y * y * y))))
    o_ref[...] = (h * gelu).astype(o_ref.dtype)


def _lru(proj, conv_w, conv_b, w_a, b_a, w_i, b_i, lam, *, batch, seq, tt=256):
    width = conv_w.shape[1]
    nt = seq // tt
    row = lambda b, t: (b * nt + t, 0)
    vec = pl.BlockSpec((1, width), lambda b, t: (0, 0))
    gate_w = pl.BlockSpec(w_a.shape, lambda b, t: (0, 0, 0))
    return pl.pallas_call(
        functools.partial(_lru_kernel, tt=tt),
        grid=(batch, nt),
        in_specs=[pl.BlockSpec((tt, width), row),
                  pl.BlockSpec((tt, width), lambda b, t: (b * nt + t, 1)),
                  pl.BlockSpec((CONV_WIDTH, width), lambda b, t: (0, 0)), vec,
                  gate_w, vec, gate_w, vec, vec],
        out_specs=pl.BlockSpec((tt, width), row),
        out_shape=jax.ShapeDtypeStruct((batch * seq, width), BF16),
        scratch_shapes=[pltpu.VMEM((tt + CONV_HALO, width), F32), pltpu.VMEM((1, width), F32)],
        compiler_params=_params("parallel", "arbitrary"),
        name="rg_lru",
    )(proj, proj, conv_w, conv_b.reshape(1, width), w_a, b_a.reshape(1, width),
      w_i, b_i.reshape(1, width), lam.reshape(1, width))


def _cumsum_rows(x):
    rows = x.shape[0]
    row = lax.broadcasted_iota(jnp.int32, x.shape, 0)
    d = 1
    while d < rows:
        x = x + jnp.where(row >= d, pltpu.roll(x, d, axis=0), 0.0)
        d *= 2
    return x


def _hgrn_kernel(q_ref, f_ref, v_ref, g_ref, lbl_ref, gn_ref, o_ref, state_ref, *, layer, chunk, n_chunks):
    @pl.when(pl.program_id(2) == 0)
    def _():
        state_ref[...] = jnp.zeros(state_ref.shape, F32)

    logits = lbl_ref[...]
    e = jnp.exp(logits - jnp.max(logits, axis=0, keepdims=True))
    lb = jnp.sum(e[0:layer + 1, :], axis=0, keepdims=True) / jnp.sum(e, axis=0, keepdims=True)

    row = lax.broadcasted_iota(jnp.int32, (chunk, HGRN_HEAD), 0)
    ti = lax.broadcasted_iota(jnp.int32, (chunk, chunk), 0)
    si = lax.broadcasted_iota(jnp.int32, (chunk, chunk), 1)
    tx = ti ^ si

    for c in range(n_chunks):
        rows = pl.ds(c * chunk, chunk)
        qr = q_ref[rows, :]
        fz = f_ref[rows, :]
        v = v_ref[rows, :].astype(BF16)
        q = qr * jax.nn.sigmoid(qr)
        log_f = jnp.log(lb + (1.0 - lb) * jax.nn.sigmoid(fz))
        kk = (1.0 - lb) * jax.nn.sigmoid(-fz)
        b = _cumsum_rows(log_f)

        scores = jnp.where(tx == 0, jnp.sum(q * kk, axis=-1, keepdims=True), 0.0)
        b_end = b
        s = 1
        while s < chunk:
            upper = (row & s) != 0
            e_q = jnp.where(upper, b - pltpu.roll(b_end, s, axis=0), NEG_BIG)
            e_k = jnp.where(upper, NEG_BIG, b_end - b)
            qd = (q * jnp.exp(e_q)).astype(BF16)
            kd = (kk * jnp.exp(e_k)).astype(BF16)
            scores = scores + jnp.where(tx < 2 * s, _dot_nt(qd, kd), 0.0)
            b_end = jnp.where(upper, b_end, pltpu.roll(b_end, chunk - s, axis=0))
            s *= 2

        state = state_ref[...]
        o = _dot(scores.astype(BF16), v) + _dot_nt((q * jnp.exp(b)).astype(BF16), state.astype(BF16))
        kd = (kk * jnp.exp(b_end - b)).astype(BF16)
        state_ref[...] = state * jnp.exp(b_end[0:1, :]) + _dot_tn(v, kd)

        o = o * lax.rsqrt(jnp.mean(o * o, axis=-1, keepdims=True) + NORM_EPS) * gn_ref[...]
        gr = g_ref[rows, :]
        o_ref[rows, :] = (o * (gr * jax.nn.sigmoid(gr))).astype(o_ref.dtype)


def _hgrn(proj, lb_logits, g_norm, *, batch, seq, layer, col0, tt=512, chunk=128):
    width = g_norm.shape[0]
    heads = width // HGRN_HEAD
    nt = seq // tt
    hb = width // HGRN_HEAD
    c0 = col0 // HGRN_HEAD

    def col(group):
        return pl.BlockSpec((tt, HGRN_HEAD), lambda b, h, t: (b * nt + t, c0 + group * hb + h))

    return pl.pallas_call(
        functools.partial(_hgrn_kernel, layer=layer, chunk=chunk, n_chunks=tt // chunk),
        grid=(batch, heads, nt),
        in_specs=[col(0), col(1), col(2), col(3),
                  pl.BlockSpec((lb_logits.shape[0], HGRN_HEAD), lambda b, h, t: (0, h)),
                  pl.BlockSpec((1, HGRN_HEAD), lambda b, h, t: (0, h))],
        out_specs=pl.BlockSpec((tt, HGRN_HEAD), lambda b, h, t: (b * nt + t, h)),
        out_shape=jax.ShapeDtypeStruct((batch * seq, width), BF16),
        scratch_shapes=[pltpu.VMEM((HGRN_HEAD, HGRN_HEAD), F32)],
        compiler_params=_params("parallel", "parallel", "arbitrary"),
        name="hgrn2",
    )(proj, proj, proj, proj, lb_logits, g_norm.reshape(1, width))


def _attn_kernel(q_ref, k_ref, v_ref, o_ref, kbuf_ref, vbuf_ref, num_ref, max_ref, den_ref, bias_ref,
                 *, tq, unroll):
    t = pl.program_id(2)

    @pl.when(t == 0)
    def _():
        kbuf_ref[0:tq, :] = jnp.zeros((tq, HEAD_DIM), F32)
        vbuf_ref[0:tq, :] = jnp.zeros((tq, HEAD_DIM), F32)

    @pl.when(t > 0)
    def _():
        kbuf_ref[0:tq, :] = kbuf_ref[tq:2 * tq, :]
        vbuf_ref[0:tq, :] = vbuf_ref[tq:2 * tq, :]

    kbuf_ref[tq:2 * tq, :] = k_ref[...]
    vbuf_ref[tq:2 * tq, :] = v_ref[...]

    blk = ATTN_BLOCK
    qi = lax.broadcasted_iota(jnp.int32, (blk, 2 * blk), 0)
    kj = lax.broadcasted_iota(jnp.int32, (blk, 2 * blk), 1)
    band = (kj >= qi) & (kj <= qi + blk)
    bias_ref[0] = jnp.where(band, 0.0, -jnp.inf)
    bias_ref[1] = jnp.where(band & (kj >= blk), 0.0, -jnp.inf)

    for dil in DILATIONS:
        per_res = tq // (blk * dil)

        def block(idx, dil=dil, per_res=per_res):
            r = idx // per_res
            n = idx % per_res
            q0 = n * (blk * dil) + r
            if dil == 1:
                q0 = pl.multiple_of(q0, blk)
                q_rows = pl.ds(q0, blk)
                k_rows = pl.ds(pl.multiple_of(tq + q0 - blk, blk), 2 * blk)
            else:
                q_rows = pl.ds(q0, blk, stride=dil)
                k_rows = pl.ds(tq + q0 - blk * dil, 2 * blk, stride=dil)
            qb = q_ref[q_rows, :].astype(BF16)
            kb = kbuf_ref[k_rows, :].astype(BF16)
            vb = vbuf_ref[k_rows, :].astype(BF16)
            first = ((t == 0) & (n == 0)).astype(jnp.int32)
            s = _dot_nt(qb, kb) + bias_ref[first]
            m = jnp.max(s, axis=-1, keepdims=True)
            p = jnp.exp(s - m)
            l = jnp.sum(p, axis=-1, keepdims=True)
            o = _dot(p.astype(BF16), vb)
            if dil == DILATIONS[0]:
                max_ref[q_rows, :] = jnp.broadcast_to(m, (blk, HEAD_DIM))
                den_ref[q_rows, :] = jnp.broadcast_to(l, (blk, HEAD_DIM))
                num_ref[q_rows, :] = o
            else:
                m_old = max_ref[q_rows, :]
                m_new = jnp.maximum(m_old, m)
                w_old = jnp.exp(m_old - m_new)
                w_new = jnp.exp(m - m_new)
                max_ref[q_rows, :] = m_new
                den_ref[q_rows, :] = den_ref[q_rows, :] * w_old + l * w_new
                num_ref[q_rows, :] = num_ref[q_rows, :] * w_old + o * w_new

        def body(i, carry, block=block):
            for u in range(unroll):
                block(i * unroll + u)
            return carry

        lax.fori_loop(0, tq // (blk * unroll), body, 0)

    o_ref[...] = (num_ref[...] / den_ref[...]).astype(o_ref.dtype)


def _attention(qkv, *, batch, seq, unroll=4):
    tq = ATTN_BLOCK * DILATIONS[-1]
    nt = seq // tq
    d_attn = ATTN_HEADS * HEAD_DIM

    def col(group):
        return pl.BlockSpec((tq, HEAD_DIM), lambda b, h, t: (b * nt + t, group * ATTN_HEADS + h))

    acc = pltpu.VMEM((tq, HEAD_DIM), F32)
    return pl.pallas_call(
        functools.partial(_attn_kernel, tq=tq, unroll=unroll),
        grid=(batch, ATTN_HEADS, nt),
        in_specs=[col(0), col(1), col(2)],
        out_specs=pl.BlockSpec((tq, HEAD_DIM), lambda b, h, t: (b * nt + t, h)),
        out_shape=jax.ShapeDtypeStruct((batch * seq, d_attn), BF16),
        scratch_shapes=[pltpu.VMEM((2 * tq, HEAD_DIM), F32), pltpu.VMEM((2 * tq, HEAD_DIM), F32),
                        acc, acc, acc, pltpu.VMEM((2, ATTN_BLOCK, 2 * ATTN_BLOCK), F32)],
        compiler_params=_params("parallel", "parallel", "arbitrary"),
        name="dilated_attention",
    )(qkv, qkv, qkv)


def kernel(x, positions, norm_mix, norm_mlp, final_norm, rec_w_in, rec_conv_w, rec_conv_b, lru_w_a, lru_b_a, lru_w_i, lru_b_i, lru_lambda, hgrn_lb_logits, hgrn_g_norm, rec_w_out, attn_w_qkv, attn_w_o, mlp_w1, mlp_w2):
    batch, seq, d = x.shape
    depth = norm_mix.shape[0]
    lru_width = rec_conv_w.shape[2]
    assert seq % (ATTN_BLOCK * DILATIONS[-1]) == 0
    h = x.reshape(batch * seq, d)
    pos = positions.reshape(batch * seq)
    for layer in range(depth):
        j = layer // 2
        if layer % 2 == 0:
            proj = _norm_matmul(h, norm_mix[layer], rec_w_in[j].astype(BF16))
            lru = _lru(proj, rec_conv_w[j], rec_conv_b[j], lru_w_a[j].astype(BF16), lru_b_a[j],
                       lru_w_i[j].astype(BF16), lru_b_i[j], lru_lambda[j], batch=batch, seq=seq)
            hg = _hgrn(proj, hgrn_lb_logits, hgrn_g_norm[j], batch=batch, seq=seq, layer=layer,
                       col0=2 * lru_width)
            w_out = rec_w_out[j].astype(BF16)
            h = _out_proj(h, [lru, hg], [w_out[:lru_width], w_out[lru_width:]])
        else:
            qkv = _qkv_proj(h, norm_mix[layer], attn_w_qkv[j].astype(BF16), pos)
            attn = _attention(qkv, batch=batch, seq=seq)
            h = _out_proj(h, [attn], [attn_w_o[j].astype(BF16)])
        last = layer == depth - 1
        h = _mlp(h, norm_mlp[layer], mlp_w1[layer].astype(BF16), mlp_w2[layer].astype(BF16),
                 final_norm if last else None)
    if depth == 0:
        h = _rms_norm(h, final_norm)
    return h.reshape(batch, seq, d)
```

```python
import functools
import math

import jax
import jax.numpy as jnp
from jax import lax
from jax.experimental import pallas as pl
from jax.experimental.pallas import tpu as pltpu

F32 = jnp.float32
BF16 = jnp.bfloat16

NORM_EPS = 1e-6
LRU_C = 8.0
LRU_HEADS = 4
CONV_WIDTH = 4
HGRN_HEAD = 128
ATTN_HEADS = 16
HEAD_DIM = 128
ROPE_DIM = 32
ROPE_THETA = 500000.0
DILATIONS = (1, 4, 16)
ATTN_BLOCK = 128
LANES = 128
QKV_CHUNK_HEADS = 2
CONV_HALO = 8
NEG_BIG = -1e30

VMEM_LIMIT = 56 * 1024 * 1024


def _params(*sem):
    return pltpu.CompilerParams(dimension_semantics=sem, vmem_limit_bytes=VMEM_LIMIT)


def _rms_norm(x, gain):
    return x * lax.rsqrt(jnp.mean(x * x, axis=-1, keepdims=True) + NORM_EPS) * gain


def _dot(a, b):
    return jnp.dot(a, b, preferred_element_type=F32)


def _dot_nt(a, b):
    return lax.dot_general(a, b, (((1,), (1,)), ((), ())), preferred_element_type=F32)


def _dot_tn(a, b):
    return lax.dot_general(a, b, (((0,), (0,)), ((), ())), preferred_element_type=F32)


def _norm_matmul_kernel(x_ref, g_ref, w_ref, o_ref, xn_ref):
    @pl.when(pl.program_id(1) == 0)
    def _():
        xn_ref[...] = _rms_norm(x_ref[...], g_ref[...]).astype(BF16)

    o_ref[...] = _dot(xn_ref[...], w_ref[...])


def _norm_matmul(x, gain, w, *, tm=1024, tn=1024):
    m, d = x.shape
    n = w.shape[1]
    return pl.pallas_call(
        _norm_matmul_kernel,
        grid=(m // tm, n // tn),
        in_specs=[pl.BlockSpec((tm, d), lambda i, j: (i, 0)),
                  pl.BlockSpec((1, d), lambda i, j: (0, 0)),
                  pl.BlockSpec((d, tn), lambda i, j: (0, j))],
        out_specs=pl.BlockSpec((tm, tn), lambda i, j: (i, j)),
        out_shape=jax.ShapeDtypeStruct((m, n), F32),
        scratch_shapes=[pltpu.VMEM((tm, d), BF16)],
        compiler_params=_params("parallel", "arbitrary"),
        name="norm_matmul",
    )(x, gain.reshape(1, d), w)


def _qkv_kernel(x_ref, g_ref, w_ref, pos_ref, freq_ref, sign_ref, o_ref, xn_ref, cos_ref, sin_ref,
                *, q_tiles, qk_tiles, heads_per_tile):
    j = pl.program_id(1)

    @pl.when(j == 0)
    def _():
        xn_ref[...] = _rms_norm(x_ref[...], g_ref[...]).astype(BF16)
        ang = pos_ref[...] * freq_ref[...]
        cos = jnp.cos(ang)
        sin = jnp.sin(ang) * sign_ref[...]
        scale = F32(HEAD_DIM ** -0.5)
        cos_ref[0] = cos * scale
        sin_ref[0] = sin * scale
        cos_ref[1] = cos
        sin_ref[1] = sin

    @pl.when(j < qk_tiles)
    def _():
        group = (j >= q_tiles).astype(jnp.int32)
        cos = cos_ref[group]
        sin = sin_ref[group]
        lane = lax.broadcasted_iota(jnp.int32, cos.shape, 1)
        half = ROPE_DIM // 2
        xn = xn_ref[...]
        for c in range(heads_per_tile // QKV_CHUNK_HEADS):
            c0 = c * QKV_CHUNK_HEADS * HEAD_DIM
            acc = _dot(xn, w_ref[:, c0:c0 + QKV_CHUNK_HEADS * HEAD_DIM])
            for hh in range(QKV_CHUNK_HEADS):
                t = acc[:, hh * HEAD_DIM:(hh + 1) * HEAD_DIM]
                partner = jnp.where(lane < half, pltpu.roll(t, HEAD_DIM - half, axis=1),
                                    pltpu.roll(t, half, axis=1))
                o_ref[:, c0 + hh * HEAD_DIM:c0 + (hh + 1) * HEAD_DIM] = t * cos + partner * sin

    @pl.when(j >= qk_tiles)
    def _():
        o_ref[...] = _dot(xn_ref[...], w_ref[...])


def _qkv_proj(x, gain, w, pos, *, tm=1024, tn=1024):
    m, d = x.shape
    n = w.shape[1]
    half = ROPE_DIM // 2
    inv_freq = 1.0 / (ROPE_THETA ** (jnp.arange(half, dtype=F32) * (2.0 / ROPE_DIM)))
    zeros = jnp.zeros((HEAD_DIM - ROPE_DIM,), F32)
    freq = jnp.concatenate([inv_freq, inv_freq, zeros]).reshape(1, HEAD_DIM)
    sign = jnp.concatenate([-jnp.ones((half,), F32), jnp.ones((half,), F32), zeros]).reshape(1, HEAD_DIM)
    d_attn = n // 3
    kern = functools.partial(_qkv_kernel, q_tiles=d_attn // tn, qk_tiles=2 * d_attn // tn,
                             heads_per_tile=tn // HEAD_DIM)
    return pl.pallas_call(
        kern,
        grid=(m // tm, n // tn),
        in_specs=[pl.BlockSpec((tm, d), lambda i, j: (i, 0)),
                  pl.BlockSpec((1, d), lambda i, j: (0, 0)),
                  pl.BlockSpec((d, tn), lambda i, j: (0, j)),
                  pl.BlockSpec((tm, 1), lambda i, j: (i, 0)),
                  pl.BlockSpec((1, HEAD_DIM), lambda i, j: (0, 0)),
                  pl.BlockSpec((1, HEAD_DIM), lambda i, j: (0, 0))],
        out_specs=pl.BlockSpec((tm, tn), lambda i, j: (i, j)),
        out_shape=jax.ShapeDtypeStruct((m, n), F32),
        scratch_shapes=[pltpu.VMEM((tm, d), BF16), pltpu.VMEM((2, tm, HEAD_DIM), F32),
                        pltpu.VMEM((2, tm, HEAD_DIM), F32)],
        compiler_params=_params("parallel", "arbitrary"),
        name="qkv_rope",
    )(x, gain.reshape(1, d), w, pos.astype(F32).reshape(m, 1), freq, sign)


def _out_proj_kernel(*refs, n_parts):
    h_ref = refs[0]
    a_refs = refs[1:1 + n_parts]
    w_refs = refs[1 + n_parts:1 + 2 * n_parts]
    o_ref = refs[1 + 2 * n_parts]
    acc = h_ref[...]
    for a_ref, w_ref in zip(a_refs, w_refs):
        acc = acc + _dot(a_ref[...], w_ref[...])
    o_ref[...] = acc


def _out_proj(h, parts, weights, *, tm=1024, tn=512):
    m, n = h.shape
    in_specs = [pl.BlockSpec((tm, tn), lambda i, j: (i, j))]
    in_specs += [pl.BlockSpec((tm, a.shape[1]), lambda i, j: (i, 0)) for a in parts]
    in_specs += [pl.BlockSpec((w.shape[0], tn), lambda i, j: (0, j)) for w in weights]
    return pl.pallas_call(
        functools.partial(_out_proj_kernel, n_parts=len(parts)),
        grid=(m // tm, n // tn),
        in_specs=in_specs,
        out_specs=pl.BlockSpec((tm, tn), lambda i, j: (i, j)),
        out_shape=jax.ShapeDtypeStruct((m, n), F32),
        compiler_params=_params("parallel", "arbitrary"),
        name="out_proj",
    )(h, *parts, *weights)


def _mlp_kernel(h_ref, g_ref, w1_ref, w2_ref, fg_ref, o_ref, xn_ref, *, final_norm):
    k = pl.program_id(1)

    @pl.when(k == 0)
    def _():
        x = h_ref[...]
        xn_ref[...] = _rms_norm(x, g_ref[...]).astype(BF16)
        o_ref[...] = x

    a = jnp.maximum(_dot(xn_ref[...], w1_ref[...]), 0.0)
    o_ref[...] += _dot((a * a).astype(BF16), w2_ref[...])

    if final_norm:
        @pl.when(k == pl.num_programs(1) - 1)
        def _():
            o_ref[...] = _rms_norm(o_ref[...], fg_ref[...])


def _mlp(h, gain, w1, w2, final_gain=None, *, tm=1024, tf=512):
    m, d = h.shape
    f = w1.shape[1]
    fg = jnp.ones((d,), F32) if final_gain is None else final_gain
    return pl.pallas_call(
        functools.partial(_mlp_kernel, final_norm=final_gain is not None),
        grid=(m // tm, f // tf),
        in_specs=[pl.BlockSpec((tm, d), lambda i, k: (i, 0)),
                  pl.BlockSpec((1, d), lambda i, k: (0, 0)),
                  pl.BlockSpec((d, tf), lambda i, k: (0, k)),
                  pl.BlockSpec((tf, d), lambda i, k: (k, 0)),
                  pl.BlockSpec((1, d), lambda i, k: (0, 0))],
        out_specs=pl.BlockSpec((tm, d), lambda i, k: (i, 0)),
        out_shape=jax.ShapeDtypeStruct((m, d), F32),
        scratch_shapes=[pltpu.VMEM((tm, d), BF16)],
        compiler_params=_params("parallel", "arbitrary"),
        name="mlp",
    )(h, gain.reshape(1, d), w1, w2, fg.reshape(1, d))


SUBLANES = 8


def _linear_scan(a, u, carry):
    rows = a.shape[0]
    sub = lax.broadcasted_iota(jnp.int32, a.shape, 0) & (SUBLANES - 1)
    d = 1
    while d < SUBLANES:
        keep = sub >= d
        u = u + a * jnp.where(keep, pltpu.roll(u, d, axis=0), 0.0)
        a = a * jnp.where(keep, pltpu.roll(a, d, axis=0), 1.0)
        d *= 2
    out = []
    for g in range(rows // SUBLANES):
        rows_g = slice(g * SUBLANES, (g + 1) * SUBLANES)
        h = u[rows_g] + a[rows_g] * carry
        carry = h[SUBLANES - 1:SUBLANES]
        out.append(h)
    return jnp.concatenate(out, axis=0), carry


def _lru_kernel(x_ref, y_ref, cw_ref, cb_ref, wa_ref, ba_ref, wi_ref, bi_ref, lam_ref, o_ref,
                xbuf_ref, carry_ref, *, tt):
    t = pl.program_id(1)

    @pl.when(t == 0)
    def _():
        xbuf_ref[0:CONV_HALO, :] = jnp.zeros((CONV_HALO, xbuf_ref.shape[1]), F32)
        carry_ref[...] = jnp.zeros(carry_ref.shape, F32)

    @pl.when(t > 0)
    def _():
        xbuf_ref[0:CONV_HALO, :] = xbuf_ref[tt:tt + CONV_HALO, :]

    xbuf_ref[CONV_HALO:, :] = x_ref[...]

    blk = xbuf_ref.shape[1] // LRU_HEADS
    for hh in range(LRU_HEADS):
        cols = slice(hh * blk, (hh + 1) * blk)
        xc = cb_ref[:, cols] + cw_ref[0:1, cols] * xbuf_ref[pl.ds(CONV_HALO - CONV_WIDTH + 1, tt), cols]
        for j in range(1, CONV_WIDTH):
            xc = xc + cw_ref[j:j + 1, cols] * xbuf_ref[pl.ds(CONV_HALO - CONV_WIDTH + 1 + j, tt), cols]

        xh = xc.astype(BF16)
        r = jax.nn.sigmoid(_dot(xh, wa_ref[hh]) + ba_ref[:, cols])
        gi = jax.nn.sigmoid(_dot(xh, wi_ref[hh]) + bi_ref[:, cols])

        neg_lam = -lam_ref[:, cols]
        softplus = jnp.maximum(neg_lam, 0.0) + jnp.log1p(jnp.exp(-jnp.abs(neg_lam)))
        log_a = (-LRU_C) * r * softplus
        a = jnp.exp(log_a)
        u = jnp.sqrt(-jnp.tanh(log_a) * (a * a + 1.0)) * (gi * xc)

        h, carry = _linear_scan(a, u, carry_ref[:, cols])
        carry_ref[:, cols] = carry

        y = y_ref[:, cols]
        gelu = 0.5 * y * (1.0 + jnp.tanh(math.sqrt(2.0 / math.pi) * (y + 0.044715 * (y * y * y))))
        o_ref[:, cols] = (h * gelu).astype(o_ref.dtype)


def _lru(proj, conv_w, conv_b, w_a, b_a, w_i, b_i, lam, *, batch, seq, tt=256):
    width = conv_w.shape[1]
    nt = seq // tt
    row = lambda b, t: (b * nt + t, 0)
    vec = pl.BlockSpec((1, width), lambda b, t: (0, 0))
    gate_w = pl.BlockSpec(w_a.shape, lambda b, t: (0, 0, 0))
    return pl.pallas_call(
        functools.partial(_lru_kernel, tt=tt),
        grid=(batch, nt),
        in_specs=[pl.BlockSpec((tt, width), row),
                  pl.BlockSpec((tt, width), lambda b, t: (b * nt + t, 1)),
                  pl.BlockSpec((CONV_WIDTH, width), lambda b, t: (0, 0)), vec,
                  gate_w, vec, gate_w, vec, vec],
        out_specs=pl.BlockSpec((tt, width), row),
        out_shape=jax.ShapeDtypeStruct((batch * seq, width), BF16),
        scratch_shapes=[pltpu.VMEM((tt + CONV_HALO, width), F32), pltpu.VMEM((1, width), F32)],
        compiler_params=_params("parallel", "arbitrary"),
        name="rg_lru",
    )(proj, proj, conv_w, conv_b.reshape(1, width), w_a, b_a.reshape(1, width),
      w_i, b_i.reshape(1, width), lam.reshape(1, width))


def _cumsum_rows(x):
    rows = x.shape[0]
    row = lax.broadcasted_iota(jnp.int32, x.shape, 0)
    d = 1
    while d < rows:
        x = x + jnp.where(row >= d, pltpu.roll(x, d, axis=0), 0.0)
        d *= 2
    return x


def _hgrn_kernel(q_ref, f_ref, v_ref, g_ref, lbl_ref, gn_ref, o_ref, state_ref, *, layer, chunk, n_chunks):
    @pl.when(pl.program_id(2) == 0)
    def _():
        state_ref[...] = jnp.zeros(state_ref.shape, F32)

    logits = lbl_ref[...]
    e = jnp.exp(logits - jnp.max(logits, axis=0, keepdims=True))
    lb = jnp.sum(e[0:layer + 1, :], axis=0, keepdims=True) / jnp.sum(e, axis=0, keepdims=True)

    row = lax.broadcasted_iota(jnp.int32, (chunk, HGRN_HEAD), 0)
    ti = lax.broadcasted_iota(jnp.int32, (chunk, chunk), 0)
    si = lax.broadcasted_iota(jnp.int32, (chunk, chunk), 1)
    tx = ti ^ si

    for c in range(n_chunks):
        rows = pl.ds(c * chunk, chunk)
        qr = q_ref[rows, :]
        fz = f_ref[rows, :]
        v = v_ref[rows, :].astype(BF16)
        q = qr * jax.nn.sigmoid(qr)
        log_f = jnp.log(lb + (1.0 - lb) * jax.nn.sigmoid(fz))
        kk = (1.0 - lb) * jax.nn.sigmoid(-fz)
        b = _cumsum_rows(log_f)

        scores = jnp.where(tx == 0, jnp.sum(q * kk, axis=-1, keepdims=True), 0.0)
        b_end = b
        s = 1
        while s < chunk:
            upper = (row & s) != 0
            e_q = jnp.where(upper, b - pltpu.roll(b_end, s, axis=0), NEG_BIG)
            e_k = jnp.where(upper, NEG_BIG, b_end - b)
            qd = (q * jnp.exp(e_q)).astype(BF16)
            kd = (kk * jnp.exp(e_k)).astype(BF16)
            scores = scores + jnp.where(tx < 2 * s, _dot_nt(qd, kd), 0.0)
            b_end = jnp.where(upper, b_end, pltpu.roll(b_end, chunk - s, axis=0))
            s *= 2

        state = state_ref[...]
        o = _dot(scores.astype(BF16), v) + _dot_nt((q * jnp.exp(b)).astype(BF16), state.astype(BF16))
        kd = (kk * jnp.exp(b_end - b)).astype(BF16)
        state_ref[...] = state * jnp.exp(b_end[0:1, :]) + _dot_tn(v, kd)

        o = o * lax.rsqrt(jnp.mean(o * o, axis=-1, keepdims=True) + NORM_EPS) * gn_ref[...]
        gr = g_ref[rows, :]
        o_ref[rows, :] = (o * (gr * jax.nn.sigmoid(gr))).astype(o_ref.dtype)


def _hgrn(proj, lb_logits, g_norm, *, batch, seq, layer, col0, tt=512, chunk=128):
    width = g_norm.shape[0]
    heads = width // HGRN_HEAD
    nt = seq // tt
    hb = width // HGRN_HEAD
    c0 = col0 // HGRN_HEAD

    def col(group):
        return pl.BlockSpec((tt, HGRN_HEAD), lambda b, h, t: (b * nt + t, c0 + group * hb + h))

    return pl.pallas_call(
        functools.partial(_hgrn_kernel, layer=layer, chunk=chunk, n_chunks=tt // chunk),
        grid=(batch, heads, nt),
        in_specs=[col(0), col(1), col(2), col(3),
                  pl.BlockSpec((lb_logits.shape[0], HGRN_HEAD), lambda b, h, t: (0, h)),
                  pl.BlockSpec((1, HGRN_HEAD), lambda b, h, t: (0, h))],
        out_specs=pl.BlockSpec((tt, HGRN_HEAD), lambda b, h, t: (b * nt + t, h)),
        out_shape=jax.ShapeDtypeStruct((batch * seq, width), BF16),
        scratch_shapes=[pltpu.VMEM((HGRN_HEAD, HGRN_HEAD), F32)],
        compiler_params=_params("parallel", "parallel", "arbitrary"),
        name="hgrn2",
    )(proj, proj, proj, proj, lb_logits, g_norm.reshape(1, width))


def _attn_kernel(q_ref, k_ref, v_ref, o_ref, kbuf_ref, vbuf_ref, num_ref, max_ref, den_ref, bias_ref,
                 *, tq, unroll):
    t = pl.program_id(2)

    @pl.when(t == 0)
    def _():
        kbuf_ref[0:tq, :] = jnp.zeros((tq, HEAD_DIM), F32)
        vbuf_ref[0:tq, :] = jnp.zeros((tq, HEAD_DIM), F32)

    @pl.when(t > 0)
    def _():
        kbuf_ref[0:tq, :] = kbuf_ref[tq:2 * tq, :]
        vbuf_ref[0:tq, :] = vbuf_ref[tq:2 * tq, :]

    kbuf_ref[tq:2 * tq, :] = k_ref[...]
    vbuf_ref[tq:2 * tq, :] = v_ref[...]

    blk = ATTN_BLOCK
    qi = lax.broadcasted_iota(jnp.int32, (blk, 2 * blk), 0)
    kj = lax.broadcasted_iota(jnp.int32, (blk, 2 * blk), 1)
    band = (kj >= qi) & (kj <= qi + blk)
    bias_ref[0] = jnp.where(band, 0.0, -jnp.inf)
    bias_ref[1] = jnp.where(band & (kj >= blk), 0.0, -jnp.inf)

    for dil in DILATIONS:
        per_res = tq // (blk * dil)

        def block(idx, dil=dil, per_res=per_res):
            r = idx // per_res
            n = idx % per_res
            q0 = n * (blk * dil) + r
            if dil == 1:
                q0 = pl.multiple_of(q0, blk)
                q_rows = pl.ds(q0, blk)
                k_rows = pl.ds(pl.multiple_of(tq + q0 - blk, blk), 2 * blk)
            else:
                q_rows = pl.ds(q0, blk, stride=dil)
                k_rows = pl.ds(tq + q0 - blk * dil, 2 * blk, stride=dil)
            qb = q_ref[q_rows, :].astype(BF16)
            kb = kbuf_ref[k_rows, :].astype(BF16)
            vb = vbuf_ref[k_rows, :].astype(BF16)
            first = ((t == 0) & (n == 0)).astype(jnp.int32)
            s = _dot_nt(qb, kb) + bias_ref[first]
            m = jnp.max(s, axis=-1, keepdims=True)
            p = jnp.exp(s - m)
            l = jnp.sum(p, axis=-1, keepdims=True)
            o = _dot(p.astype(BF16), vb)
            if dil == DILATIONS[0]:
                max_ref[q_rows, :] = jnp.broadcast_to(m, (blk, HEAD_DIM))
                den_ref[q_rows, :] = jnp.broadcast_to(l, (blk, HEAD_DIM))
                num_ref[q_rows, :] = o
            else:
                m_old = max_ref[q_rows, :]
                m_new = jnp.maximum(m_old, m)
                w_old = jnp.exp(m_old - m_new)
                w_new = jnp.exp(m - m_new)
                max_ref[q_rows, :] = m_new
                den_ref[q_rows, :] = den_ref[q_rows, :] * w_old + l * w_new
                num_ref[q_rows, :] = num_ref[q_rows, :] * w_old + o * w_new

        def body(i, carry, block=block):
            for u in range(unroll):
                block(i * unroll + u)
            return carry

        lax.fori_loop(0, tq // (blk * unroll), body, 0)

    o_ref[...] = (num_ref[...] / den_ref[...]).astype(o_ref.dtype)


def _attention(qkv, *, batch, seq, unroll=8):
    tq = ATTN_BLOCK * DILATIONS[-1]
    nt = seq // tq
    d_attn = ATTN_HEADS * HEAD_DIM

    def col(group):
        return pl.BlockSpec((tq, HEAD_DIM), lambda b, h, t: (b * nt + t, group * ATTN_HEADS + h))

    acc = pltpu.VMEM((tq, HEAD_DIM), F32)
    return pl.pallas_call(
        functools.partial(_attn_kernel, tq=tq, unroll=unroll),
        grid=(batch, ATTN_HEADS, nt),
        in_specs=[col(0), col(1), col(2)],
        out_specs=pl.BlockSpec((tq, HEAD_DIM), lambda b, h, t: (b * nt + t, h)),
        out_shape=jax.ShapeDtypeStruct((batch * seq, d_attn), BF16),
        scratch_shapes=[pltpu.VMEM((2 * tq, HEAD_DIM), F32), pltpu.VMEM((2 * tq, HEAD_DIM), F32),
                        acc, acc, acc, pltpu.VMEM((2, ATTN_BLOCK, 2 * ATTN_BLOCK), F32)],
        compiler_params=_params("parallel", "parallel", "arbitrary"),
        name="dilated_attention",
    )(qkv, qkv, qkv)


def kernel(x, positions, norm_mix, norm_mlp, final_norm, rec_w_in, rec_conv_w, rec_conv_b, lru_w_a, lru_b_a, lru_w_i, lru_b_i, lru_lambda, hgrn_lb_logits, hgrn_g_norm, rec_w_out, attn_w_qkv, attn_w_o, mlp_w1, mlp_w2):
    batch, seq, d = x.shape
    depth = norm_mix.shape[0]
    lru_width = rec_conv_w.shape[2]
    assert seq % (ATTN_BLOCK * DILATIONS[-1]) == 0
    h = x.reshape(batch * seq, d)
    pos = positions.reshape(batch * seq)
    for layer in range(depth):
        j = layer // 2
        if layer % 2 == 0:
            proj = _norm_matmul(h, norm_mix[layer], rec_w_in[j].astype(BF16))
            lru = _lru(proj, rec_conv_w[j], rec_conv_b[j], lru_w_a[j].astype(BF16), lru_b_a[j],
                       lru_w_i[j].astype(BF16), lru_b_i[j], lru_lambda[j], batch=batch, seq=seq)
            hg = _hgrn(proj, hgrn_lb_logits, hgrn_g_norm[j], batch=batch, seq=seq, layer=layer,
                       col0=2 * lru_width)
            w_out = rec_w_out[j].astype(BF16)
            h = _out_proj(h, [lru, hg], [w_out[:lru_width], w_out[lru_width:]])
        else:
            qkv = _qkv_proj(h, norm_mix[layer], attn_w_qkv[j].astype(BF16), pos)
            attn = _attention(qkv, batch=batch, seq=seq)
            h = _out_proj(h, [attn], [attn_w_o[j].astype(BF16)])
        last = layer == depth - 1
        h = _mlp(h, norm_mlp[layer], mlp_w1[layer].astype(BF16), mlp_w2[layer].astype(BF16),
                 final_norm if last else None)
    if depth == 0:
        h = _rms_norm(h, final_norm)
    return h.reshape(batch, seq, d)
```

```python
import functools
import math

import jax
import jax.numpy as jnp
from jax import lax
from jax.experimental import pallas as pl
from jax.experimental.pallas import tpu as pltpu

F32 = jnp.float32
BF16 = jnp.bfloat16

NORM_EPS = 1e-6
LRU_C = 8.0
LRU_HEADS = 4
CONV_WIDTH = 4
HGRN_HEAD = 128
ATTN_HEADS = 16
HEAD_DIM = 128
ROPE_DIM = 32
ROPE_THETA = 500000.0
DILATIONS = (1, 4, 16)
ATTN_BLOCK = 128
ATTN_RES = 16
LANES = 128
QKV_CHUNK_HEADS = 2
CONV_HALO = 8

VMEM_LIMIT = 56 * 1024 * 1024


def _params(*sem):
    return pltpu.CompilerParams(dimension_semantics=sem, vmem_limit_bytes=VMEM_LIMIT)


def _rms_norm(x, gain):
    return x * lax.rsqrt(jnp.mean(x * x, axis=-1, keepdims=True) + NORM_EPS) * gain


def _dot(a, b):
    return jnp.dot(a, b, preferred_element_type=F32)


def _dot_nt(a, b):
    return lax.dot_general(a, b, (((1,), (1,)), ((), ())), preferred_element_type=F32)


def _dot_tn(a, b):
    return lax.dot_general(a, b, (((0,), (0,)), ((), ())), preferred_element_type=F32)


def _norm_matmul_kernel(x_ref, g_ref, w_ref, o_ref, xn_ref):
    @pl.when(pl.program_id(1) == 0)
    def _():
        xn_ref[...] = _rms_norm(x_ref[...], g_ref[...]).astype(BF16)

    o_ref[...] = _dot(xn_ref[...], w_ref[...])


def _norm_matmul(x, gain, w, *, tm=1024, tn=1024):
    m, d = x.shape
    n = w.shape[1]
    return pl.pallas_call(
        _norm_matmul_kernel,
        grid=(m // tm, n // tn),
        in_specs=[pl.BlockSpec((tm, d), lambda i, j: (i, 0)),
                  pl.BlockSpec((1, d), lambda i, j: (0, 0)),
                  pl.BlockSpec((d, tn), lambda i, j: (0, j))],
        out_specs=pl.BlockSpec((tm, tn), lambda i, j: (i, j)),
        out_shape=jax.ShapeDtypeStruct((m, n), F32),
        scratch_shapes=[pltpu.VMEM((tm, d), BF16)],
        compiler_params=_params("parallel", "arbitrary"),
        name="norm_matmul",
    )(x, gain.reshape(1, d), w)


def _qkv_kernel(x_ref, g_ref, w_ref, pos_ref, freq_ref, sign_ref, o_ref, xn_ref, cos_ref, sin_ref,
                *, q_tiles, qk_tiles, heads_per_tile):
    j = pl.program_id(1)

    @pl.when(j == 0)
    def _():
        xn_ref[...] = _rms_norm(x_ref[...], g_ref[...]).astype(BF16)
        ang = pos_ref[...] * freq_ref[...]
        cos = jnp.cos(ang)
        sin = jnp.sin(ang) * sign_ref[...]
        scale = F32(HEAD_DIM ** -0.5)
        cos_ref[0] = cos * scale
        sin_ref[0] = sin * scale
        cos_ref[1] = cos
        sin_ref[1] = sin

    @pl.when(j < qk_tiles)
    def _():
        group = (j >= q_tiles).astype(jnp.int32)
        cos = cos_ref[group]
        sin = sin_ref[group]
        lane = lax.broadcasted_iota(jnp.int32, cos.shape, 1)
        half = ROPE_DIM // 2
        xn = xn_ref[...]
        for c in range(heads_per_tile // QKV_CHUNK_HEADS):
            c0 = c * QKV_CHUNK_HEADS * HEAD_DIM
            acc = _dot(xn, w_ref[:, c0:c0 + QKV_CHUNK_HEADS * HEAD_DIM])
            for hh in range(QKV_CHUNK_HEADS):
                t = acc[:, hh * HEAD_DIM:(hh + 1) * HEAD_DIM]
                partner = jnp.where(lane < half, pltpu.roll(t, HEAD_DIM - half, axis=1),
                                    pltpu.roll(t, half, axis=1))
                o_ref[:, c0 + hh * HEAD_DIM:c0 + (hh + 1) * HEAD_DIM] = t * cos + partner * sin

    @pl.when(j >= qk_tiles)
    def _():
        o_ref[...] = _dot(xn_ref[...], w_ref[...])


def _qkv_proj(x, gain, w, pos, *, tm=1024, tn=1024):
    m, d = x.shape
    n = w.shape[1]
    half = ROPE_DIM // 2
    inv_freq = 1.0 / (ROPE_THETA ** (jnp.arange(half, dtype=F32) * (2.0 / ROPE_DIM)))
    zeros = jnp.zeros((HEAD_DIM - ROPE_DIM,), F32)
    freq = jnp.concatenate([inv_freq, inv_freq, zeros]).reshape(1, HEAD_DIM)
    sign = jnp.concatenate([-jnp.ones((half,), F32), jnp.ones((half,), F32), zeros]).reshape(1, HEAD_DIM)
    d_attn = n // 3
    kern = functools.partial(_qkv_kernel, q_tiles=d_attn // tn, qk_tiles=2 * d_attn // tn,
                             heads_per_tile=tn // HEAD_DIM)
    return pl.pallas_call(
        kern,
        grid=(m // tm, n // tn),
        in_specs=[pl.BlockSpec((tm, d), lambda i, j: (i, 0)),
                  pl.BlockSpec((1, d), lambda i, j: (0, 0)),
                  pl.BlockSpec((d, tn), lambda i, j: (0, j)),
                  pl.BlockSpec((tm, 1), lambda i, j: (i, 0)),
                  pl.BlockSpec((1, HEAD_DIM), lambda i, j: (0, 0)),
                  pl.BlockSpec((1, HEAD_DIM), lambda i, j: (0, 0))],
        out_specs=pl.BlockSpec((tm, tn), lambda i, j: (i, j)),
        out_shape=jax.ShapeDtypeStruct((m, n), F32),
        scratch_shapes=[pltpu.VMEM((tm, d), BF16), pltpu.VMEM((2, tm, HEAD_DIM), F32),
                        pltpu.VMEM((2, tm, HEAD_DIM), F32)],
        compiler_params=_params("parallel", "arbitrary"),
        name="qkv_rope",
    )(x, gain.reshape(1, d), w, pos.astype(F32).reshape(m, 1), freq, sign)


def _out_proj_kernel(*refs, n_parts):
    h_ref = refs[0]
    a_refs = refs[1:1 + n_parts]
    w_refs = refs[1 + n_parts:1 + 2 * n_parts]
    o_ref = refs[1 + 2 * n_parts]
    acc = h_ref[...]
    for a_ref, w_ref in zip(a_refs, w_refs):
        acc = acc + _dot(a_ref[...].astype(BF16), w_ref[...])
    o_ref[...] = acc


def _out_proj(h, parts, w, *, tm=512):
    m, n = h.shape
    k_part = parts[0].shape[1]
    assert all(a.shape[1] == k_part for a in parts) and k_part * len(parts) == w.shape[0]
    row = pl.BlockSpec((tm, n), lambda i: (i, 0))
    in_specs = [row]
    in_specs += [pl.BlockSpec((tm, k_part), lambda i: (i, 0)) for _ in parts]
    in_specs += [pl.BlockSpec((k_part, n), lambda i, p=p: (p, 0)) for p in range(len(parts))]
    return pl.pallas_call(
        functools.partial(_out_proj_kernel, n_parts=len(parts)),
        grid=(m // tm,),
        in_specs=in_specs,
        out_specs=row,
        out_shape=jax.ShapeDtypeStruct((m, n), F32),
        compiler_params=_params("parallel"),
        name="out_proj",
    )(h, *parts, *([w] * len(parts)))


def _mlp_kernel(h_ref, g_ref, w1_ref, w2_ref, fg_ref, o_ref, xn_ref, *, final_norm):
    k = pl.program_id(1)

    @pl.when(k == 0)
    def _():
        x = h_ref[...]
        xn_ref[...] = _rms_norm(x, g_ref[...]).astype(BF16)
        o_ref[...] = x

    a = jnp.maximum(_dot(xn_ref[...], w1_ref[...]), 0.0)
    o_ref[...] += _dot((a * a).astype(BF16), w2_ref[...])

    if final_norm:
        @pl.when(k == pl.num_programs(1) - 1)
        def _():
            o_ref[...] = _rms_norm(o_ref[...], fg_ref[...])


def _mlp(h, gain, w1, w2, layer, final_gain=None, *, tm=1024, tf=512):
    m, d = h.shape
    f = w1.shape[2]
    fg = jnp.ones((d,), F32) if final_gain is None else final_gain
    return pl.pallas_call(
        functools.partial(_mlp_kernel, final_norm=final_gain is not None),
        grid=(m // tm, f // tf),
        in_specs=[pl.BlockSpec((tm, d), lambda i, k: (i, 0)),
                  pl.BlockSpec((1, d), lambda i, k: (0, 0)),
                  pl.BlockSpec((None, d, tf), lambda i, k: (layer, 0, k)),
                  pl.BlockSpec((None, tf, d), lambda i, k: (layer, k, 0)),
                  pl.BlockSpec((1, d), lambda i, k: (0, 0))],
        out_specs=pl.BlockSpec((tm, d), lambda i, k: (i, 0)),
        out_shape=jax.ShapeDtypeStruct((m, d), F32),
        scratch_shapes=[pltpu.VMEM((tm, d), BF16)],
        compiler_params=_params("parallel", "arbitrary"),
        name="mlp",
    )(h, gain.reshape(1, d), w1, w2, fg.reshape(1, d))


SUBLANES = 8


def _linear_scan(a, u, carry):
    rows = a.shape[0]
    sub = lax.broadcasted_iota(jnp.int32, a.shape, 0) & (SUBLANES - 1)
    d = 1
    while d < SUBLANES:
        keep = sub >= d
        u = u + a * jnp.where(keep, pltpu.roll(u, d, axis=0), 0.0)
        a = a * jnp.where(keep, pltpu.roll(a, d, axis=0), 1.0)
        d *= 2
    out = []
    for g in range(rows // SUBLANES):
        rows_g = slice(g * SUBLANES, (g + 1) * SUBLANES)
        h = u[rows_g] + a[rows_g] * carry
        carry = h[SUBLANES - 1:SUBLANES]
        out.append(h)
    return jnp.concatenate(out, axis=0), carry


def _lru_kernel(x_ref, y_ref, cw_ref, cb_ref, wa_ref, ba_ref, wi_ref, bi_ref, lam_ref, o_ref,
                xbuf_ref, carry_ref, *, tt):
    t = pl.program_id(1)

    @pl.when(t == 0)
    def _():
        xbuf_ref[0:CONV_HALO, :] = jnp.zeros((CONV_HALO, xbuf_ref.shape[1]), F32)
        carry_ref[...] = jnp.zeros(carry_ref.shape, F32)

    @pl.when(t > 0)
    def _():
        xbuf_ref[0:CONV_HALO, :] = xbuf_ref[tt:tt + CONV_HALO, :]

    xbuf_ref[CONV_HALO:, :] = x_ref[...]

    blk = xbuf_ref.shape[1] // LRU_HEADS
    for hh in range(LRU_HEADS):
        cols = slice(hh * blk, (hh + 1) * blk)
        xc = cb_ref[:, cols] + cw_ref[0:1, cols] * xbuf_ref[pl.ds(CONV_HALO - CONV_WIDTH + 1, tt), cols]
        for j in range(1, CONV_WIDTH):
            xc = xc + cw_ref[j:j + 1, cols] * xbuf_ref[pl.ds(CONV_HALO - CONV_WIDTH + 1 + j, tt), cols]

        xh = xc.astype(BF16)
        r = jax.nn.sigmoid(_dot(xh, wa_ref[hh]) + ba_ref[:, cols])
        gi = jax.nn.sigmoid(_dot(xh, wi_ref[hh]) + bi_ref[:, cols])

        neg_lam = -lam_ref[:, cols]
        softplus = jnp.maximum(neg_lam, 0.0) + jnp.log1p(jnp.exp(-jnp.abs(neg_lam)))
        log_a = (-LRU_C) * r * softplus
        a = jnp.exp(log_a)
        u = jnp.sqrt(-jnp.tanh(log_a) * (a * a + 1.0)) * (gi * xc)

        h, carry = _linear_scan(a, u, carry_ref[:, cols])
        carry_ref[:, cols] = carry

        y = y_ref[:, cols]
        gelu = 0.5 * y * (1.0 + jnp.tanh(math.sqrt(2.0 / math.pi) * (y + 0.044715 * (y * y * y))))
        o_ref[:, cols] = (h * gelu).astype(o_ref.dtype)


def _lru(proj, conv_w, conv_b, w_a, b_a, w_i, b_i, lam, *, batch, seq, tt=256):
    width = conv_w.shape[1]
    nt = seq // tt
    row = lambda b, t: (b * nt + t, 0)
    vec = pl.BlockSpec((1, width), lambda b, t: (0, 0))
    gate_w = pl.BlockSpec(w_a.shape, lambda b, t: (0, 0, 0))
    return pl.pallas_call(
        functools.partial(_lru_kernel, tt=tt),
        grid=(batch, nt),
        in_specs=[pl.BlockSpec((tt, width), row),
                  pl.BlockSpec((tt, width), lambda b, t: (b * nt + t, 1)),
                  pl.BlockSpec((CONV_WIDTH, width), lambda b, t: (0, 0)), vec,
                  gate_w, vec, gate_w, vec, vec],
        out_specs=pl.BlockSpec((tt, width), row),
        out_shape=jax.ShapeDtypeStruct((batch * seq, width), BF16),
        scratch_shapes=[pltpu.VMEM((tt + CONV_HALO, width), F32), pltpu.VMEM((1, width), F32)],
        compiler_params=_params("parallel", "arbitrary"),
        name="rg_lru",
    )(proj, proj, conv_w, conv_b.reshape(1, width), w_a, b_a.reshape(1, width),
      w_i, b_i.reshape(1, width), lam.reshape(1, width))


def _cumsum_rows(x):
    rows = x.shape[0]
    row = lax.broadcasted_iota(jnp.int32, x.shape, 0)
    d = 1
    while d < rows:
        x = x + jnp.where(row >= d, pltpu.roll(x, d, axis=0), 0.0)
        d *= 2
    return x


def _hgrn_kernel(q_ref, f_ref, v_ref, g_ref, lbl_ref, gn_ref, o_ref, state_ref, level_ref,
                 *, layer, chunk, n_chunks):
    ti = lax.broadcasted_iota(jnp.int32, (chunk, chunk), 0)
    si = lax.broadcasted_iota(jnp.int32, (chunk, chunk), 1)
    tx = ti ^ si

    @pl.when(pl.program_id(2) == 0)
    def _():
        state_ref[...] = jnp.zeros(state_ref.shape, F32)
        for li in range(level_ref.shape[0]):
            s = 1 << li
            level_ref[li] = jnp.where((tx >= s) & (tx < 2 * s) & (ti > si), 1.0, 0.0)

    logits = lbl_ref[...]
    e = jnp.exp(logits - jnp.max(logits, axis=0, keepdims=True))
    lb = jnp.sum(e[0:layer + 1, :], axis=0, keepdims=True) / jnp.sum(e, axis=0, keepdims=True)

    row = lax.broadcasted_iota(jnp.int32, (chunk, HGRN_HEAD), 0)

    for c in range(n_chunks):
        rows = pl.ds(c * chunk, chunk)
        qr = q_ref[rows, :]
        fz = f_ref[rows, :]
        v = v_ref[rows, :].astype(BF16)
        q = qr * jax.nn.sigmoid(qr)
        log_f = jnp.log(lb + (1.0 - lb) * jax.nn.sigmoid(fz))
        kk = (1.0 - lb) * jax.nn.sigmoid(-fz)
        b = _cumsum_rows(log_f)

        scores = jnp.where(tx == 0, jnp.sum(q * kk, axis=-1, keepdims=True), 0.0)
        b_end = b
        s = 1
        for li in range(level_ref.shape[0]):
            upper = (row & s) != 0
            decay = jnp.exp(jnp.where(upper, b - pltpu.roll(b_end, s, axis=0), b_end - b))
            z = (jnp.where(upper, q, kk) * decay).astype(BF16)
            scores = scores + _dot_nt(z, z) * level_ref[li]
            b_end = jnp.where(upper, b_end, pltpu.roll(b_end, chunk - s, axis=0))
            s *= 2

        state = state_ref[...]
        o = _dot(scores.astype(BF16), v) + _dot_nt((q * jnp.exp(b)).astype(BF16), state.astype(BF16))
        kd = (kk * jnp.exp(b_end - b)).astype(BF16)
        state_ref[...] = state * jnp.exp(b_end[0:1, :]) + _dot_tn(v, kd)

        o = o * lax.rsqrt(jnp.mean(o * o, axis=-1, keepdims=True) + NORM_EPS) * gn_ref[...]
        gr = g_ref[rows, :]
        o_ref[rows, :] = (o * (gr * jax.nn.sigmoid(gr))).astype(o_ref.dtype)


def _hgrn(proj, lb_logits, g_norm, *, batch, seq, layer, col0, tt=512, chunk=128):
    width = g_norm.shape[0]
    heads = width // HGRN_HEAD
    nt = seq // tt
    hb = width // HGRN_HEAD
    c0 = col0 // HGRN_HEAD

    def col(group):
        return pl.BlockSpec((tt, HGRN_HEAD), lambda b, h, t: (b * nt + t, c0 + group * hb + h))

    return pl.pallas_call(
        functools.partial(_hgrn_kernel, layer=layer, chunk=chunk, n_chunks=tt // chunk),
        grid=(batch, heads, nt),
        in_specs=[col(0), col(1), col(2), col(3),
                  pl.BlockSpec((lb_logits.shape[0], HGRN_HEAD), lambda b, h, t: (0, h)),
                  pl.BlockSpec((1, HGRN_HEAD), lambda b, h, t: (0, h))],
        out_specs=pl.BlockSpec((tt, HGRN_HEAD), lambda b, h, t: (b * nt + t, h)),
        out_shape=jax.ShapeDtypeStruct((batch * seq, width), BF16),
        scratch_shapes=[pltpu.VMEM((HGRN_HEAD, HGRN_HEAD), F32),
                        pltpu.VMEM((chunk.bit_length() - 1, chunk, chunk), F32)],
        compiler_params=_params("parallel", "parallel", "arbitrary"),
        name="hgrn2",
    )(proj, proj, proj, proj, lb_logits, g_norm.reshape(1, width))


def _attn_kernel(q_ref, k_ref, v_ref, o_ref, qc_ref, kbuf_ref, vbuf_ref, num_ref, max_ref, den_ref, bias_ref,
                 *, unroll):
    t = pl.program_id(2)
    blk = ATTN_BLOCK
    res = ATTN_RES

    @pl.when(t == 0)
    def _():
        kbuf_ref[:, 0:blk, :] = jnp.zeros((res, blk, HEAD_DIM), F32)
        vbuf_ref[:, 0:blk, :] = jnp.zeros((res, blk, HEAD_DIM), F32)
        a = lax.broadcasted_iota(jnp.int32, (blk, 2 * blk), 0)
        c = lax.broadcasted_iota(jnp.int32, (blk, 2 * blk), 1)
        for bi, dil in enumerate(DILATIONS):
            runs = res // dil
            q_run = blk // runs
            k_run = 2 * blk // runs
            dist = blk + runs * (a % q_run - c % k_run) + (a // q_run - c // k_run)
            ok = (dist >= 0) & (dist <= blk)
            bias_ref[2 * bi] = jnp.where(ok, 0.0, -jnp.inf)
            bias_ref[2 * bi + 1] = jnp.where(ok & (c % k_run >= q_run), 0.0, -jnp.inf)

    @pl.when(t > 0)
    def _():
        kbuf_ref[:, 0:blk, :] = kbuf_ref[:, blk:2 * blk, :]
        vbuf_ref[:, 0:blk, :] = vbuf_ref[:, blk:2 * blk, :]

    for r in range(res):
        rows = pl.ds(r, blk, stride=res)
        qc_ref[r] = q_ref[rows, :]
        kbuf_ref[r, blk:2 * blk, :] = k_ref[rows, :]
        vbuf_ref[r, blk:2 * blk, :] = v_ref[rows, :]

    for bi, dil in enumerate(DILATIONS):
        runs = res // dil
        q_run = blk // runs
        k_run = 2 * blk // runs

        def block(idx, bi=bi, dil=dil, runs=runs, q_run=q_run, k_run=k_run):
            r_d = idx // runs
            n = idx % runs
            q0 = pl.multiple_of(n * q_run, q_run)
            k0 = pl.multiple_of(blk + (n - 1) * q_run, q_run)
            planes = [r_d + dil * j for j in range(runs)]
            qb = jnp.concatenate([qc_ref[p, pl.ds(q0, q_run), :] for p in planes], axis=0).astype(BF16)
            kb = jnp.concatenate([kbuf_ref[p, pl.ds(k0, k_run), :] for p in planes], axis=0).astype(BF16)
            vb = jnp.concatenate([vbuf_ref[p, pl.ds(k0, k_run), :] for p in planes], axis=0).astype(BF16)
            first = ((t == 0) & (n == 0)).astype(jnp.int32)
            s = _dot_nt(qb, kb) + bias_ref[2 * bi + first]
            m = jnp.max(s, axis=-1, keepdims=True)
            p = jnp.exp(s - m)
            l = jnp.sum(p, axis=-1, keepdims=True)
            o = _dot(p.astype(BF16), vb)
            for j, plane in enumerate(planes):
                run = slice(j * q_run, (j + 1) * q_run)
                dst = (plane, pl.ds(q0, q_run), slice(None))
                m_j = jnp.broadcast_to(m[run], (q_run, HEAD_DIM))
                l_j = jnp.broadcast_to(l[run], (q_run, HEAD_DIM))
                if bi == 0:
                    max_ref[dst] = m_j
                    den_ref[dst] = l_j
                    num_ref[dst] = o[run]
                else:
                    m_old = max_ref[dst]
                    m_new = jnp.maximum(m_old, m_j)
                    w_old = jnp.exp(m_old - m_new)
                    w_new = jnp.exp(m_j - m_new)
                    max_ref[dst] = m_new
                    den_ref[dst] = den_ref[dst] * w_old + l_j * w_new
                    num_ref[dst] = num_ref[dst] * w_old + o[run] * w_new

        def body(i, carry, block=block):
            for u in range(unroll):
                block(i * unroll + u)
            return carry

        lax.fori_loop(0, res // unroll, body, 0)

    for r in range(res):
        o_ref[pl.ds(r, blk, stride=res), :] = num_ref[r] / den_ref[r]


def _attention(qkv, *, batch, seq, unroll=8):
    tq = ATTN_BLOCK * ATTN_RES
    nt = seq // tq
    d_attn = ATTN_HEADS * HEAD_DIM

    def col(group):
        return pl.BlockSpec((tq, HEAD_DIM), lambda b, h, t: (b * nt + t, group * ATTN_HEADS + h))

    plane = pltpu.VMEM((ATTN_RES, ATTN_BLOCK, HEAD_DIM), F32)
    band = pltpu.VMEM((ATTN_RES, 2 * ATTN_BLOCK, HEAD_DIM), F32)
    return pl.pallas_call(
        functools.partial(_attn_kernel, unroll=unroll),
        grid=(batch, ATTN_HEADS, nt),
        in_specs=[col(0), col(1), col(2)],
        out_specs=pl.BlockSpec((tq, HEAD_DIM), lambda b, h, t: (b * nt + t, h)),
        out_shape=jax.ShapeDtypeStruct((batch * seq, d_attn), F32),
        scratch_shapes=[plane, band, band, plane, plane, plane,
                        pltpu.VMEM((2 * len(DILATIONS), ATTN_BLOCK, 2 * ATTN_BLOCK), F32)],
        compiler_params=_params("parallel", "parallel", "arbitrary"),
        name="dilated_attention",
    )(qkv, qkv, qkv)


def kernel(x, positions, norm_mix, norm_mlp, final_norm, rec_w_in, rec_conv_w, rec_conv_b, lru_w_a, lru_b_a, lru_w_i, lru_b_i, lru_lambda, hgrn_lb_logits, hgrn_g_norm, rec_w_out, attn_w_qkv, attn_w_o, mlp_w1, mlp_w2):
    batch, seq, d = x.shape
    depth = norm_mix.shape[0]
    lru_width = rec_conv_w.shape[2]
    assert seq % (ATTN_BLOCK * ATTN_RES) == 0 and DILATIONS[-1] == ATTN_RES
    h = x.reshape(batch * seq, d)
    pos = positions.reshape(batch * seq)
    mlp_w1 = mlp_w1.astype(BF16)
    mlp_w2 = mlp_w2.astype(BF16)
    for layer in range(depth):
        j = layer // 2
        if layer % 2 == 0:
            proj = _norm_matmul(h, norm_mix[layer], rec_w_in[j].astype(BF16))
            lru = _lru(proj, rec_conv_w[j], rec_conv_b[j], lru_w_a[j].astype(BF16), lru_b_a[j],
                       lru_w_i[j].astype(BF16), lru_b_i[j], lru_lambda[j], batch=batch, seq=seq)
            hg = _hgrn(proj, hgrn_lb_logits, hgrn_g_norm[j], batch=batch, seq=seq, layer=layer,
                       col0=2 * lru_width)
            h = _out_proj(h, [lru, hg], rec_w_out[j].astype(BF16))
        else:
            qkv = _qkv_proj(h, norm_mix[layer], attn_w_qkv[j].astype(BF16), pos)
            attn = _attention(qkv, batch=batch, seq=seq)
            h = _out_proj(h, [attn], attn_w_o[j].astype(BF16))
        last = layer == depth - 1
        h = _mlp(h, norm_mlp[layer], mlp_w1, mlp_w2, layer, final_norm if last else None)
    if depth == 0:
        h = _rms_norm(h, final_norm)
    return h.reshape(batch, seq, d)
```

```python
import functools
import math

import jax
import jax.numpy as jnp
from jax import lax
from jax.experimental import pallas as pl
from jax.experimental.pallas import tpu as pltpu

F32 = jnp.float32
BF16 = jnp.bfloat16

NORM_EPS = 1e-6
LRU_C = 8.0
LRU_HEADS = 4
CONV_WIDTH = 4
HGRN_HEAD = 128
ATTN_HEADS = 16
HEAD_DIM = 128
ROPE_DIM = 32
ROPE_THETA = 500000.0
DILATIONS = (1, 4, 16)
ATTN_BLOCK = 128
ATTN_RES = 16
LANES = 128
NORM_ROW_CHUNK = 256
QKV_CHUNK_HEADS = 2
CONV_HALO = 8

VMEM_LIMIT = 56 * 1024 * 1024


def _params(*sem):
    return pltpu.CompilerParams(dimension_semantics=sem, vmem_limit_bytes=VMEM_LIMIT)


def _rms_norm(x, gain):
    return x * lax.rsqrt(jnp.mean(x * x, axis=-1, keepdims=True) + NORM_EPS) * gain


def _dot(a, b):
    return jnp.dot(a, b, preferred_element_type=F32)


def _dot_nt(a, b):
    return lax.dot_general(a, b, (((1,), (1,)), ((), ())), preferred_element_type=F32)


def _dot_tn(a, b):
    return lax.dot_general(a, b, (((0,), (0,)), ((), ())), preferred_element_type=F32)


def _norm_matmul_kernel(x_ref, g_ref, w_ref, o_ref, xn_ref):
    j = pl.program_id(1)

    @pl.when(j == 0)
    def _():
        for r in range(0, x_ref.shape[0], NORM_ROW_CHUNK):
            rows = pl.ds(r, NORM_ROW_CHUNK)
            xn = _rms_norm(x_ref[rows, :], g_ref[...]).astype(BF16)
            xn_ref[rows, :] = xn
            o_ref[rows, :] = _dot(xn, w_ref[...])

    @pl.when(j > 0)
    def _():
        o_ref[...] = _dot(xn_ref[...], w_ref[...])


def _norm_matmul(x, gain, w, *, tm=1024, tn=1024):
    m, d = x.shape
    n = w.shape[1]
    return pl.pallas_call(
        _norm_matmul_kernel,
        grid=(m // tm, n // tn),
        in_specs=[pl.BlockSpec((tm, d), lambda i, j: (i, 0)),
                  pl.BlockSpec((1, d), lambda i, j: (0, 0)),
                  pl.BlockSpec((d, tn), lambda i, j: (0, j))],
        out_specs=pl.BlockSpec((tm, tn), lambda i, j: (i, j)),
        out_shape=jax.ShapeDtypeStruct((m, n), F32),
        scratch_shapes=[pltpu.VMEM((tm, d), BF16)],
        compiler_params=_params("parallel", "arbitrary"),
        name="norm_matmul",
    )(x, gain.reshape(1, d), w)


def _qkv_kernel(x_ref, g_ref, w_ref, pos_ref, freq_ref, sign_ref, o_ref, xn_ref, cos_ref, sin_ref,
                *, q_tiles, qk_tiles, heads_per_tile):
    j = pl.program_id(1)
    half = ROPE_DIM // 2

    def rotated_tile(rows, xn, cos, sin):
        lane = lax.broadcasted_iota(jnp.int32, cos.shape, 1)
        for c in range(heads_per_tile // QKV_CHUNK_HEADS):
            c0 = c * QKV_CHUNK_HEADS * HEAD_DIM
            acc = _dot(xn, w_ref[:, c0:c0 + QKV_CHUNK_HEADS * HEAD_DIM])
            for hh in range(QKV_CHUNK_HEADS):
                t = acc[:, hh * HEAD_DIM:(hh + 1) * HEAD_DIM]
                partner = jnp.where(lane < half, pltpu.roll(t, HEAD_DIM - half, axis=1),
                                    pltpu.roll(t, half, axis=1))
                o_ref[rows, c0 + hh * HEAD_DIM:c0 + (hh + 1) * HEAD_DIM] = t * cos + partner * sin

    @pl.when(j == 0)
    def _():
        for r in range(0, x_ref.shape[0], NORM_ROW_CHUNK):
            rows = pl.ds(r, NORM_ROW_CHUNK)
            xn = _rms_norm(x_ref[rows, :], g_ref[...]).astype(BF16)
            xn_ref[rows, :] = xn
            ang = pos_ref[rows, :] * freq_ref[...]
            cos = jnp.cos(ang)
            sin = jnp.sin(ang) * sign_ref[...]
            scale = F32(HEAD_DIM ** -0.5 * math.log2(math.e))
            cos_ref[0, rows, :] = cos * scale
            sin_ref[0, rows, :] = sin * scale
            cos_ref[1, rows, :] = cos
            sin_ref[1, rows, :] = sin
            rotated_tile(rows, xn, cos * scale, sin * scale)

    @pl.when((j > 0) & (j < qk_tiles))
    def _():
        group = (j >= q_tiles).astype(jnp.int32)
        rotated_tile(slice(None), xn_ref[...], cos_ref[group], sin_ref[group])

    @pl.when(j >= qk_tiles)
    def _():
        o_ref[...] = _dot(xn_ref[...], w_ref[...])


def _qkv_proj(x, gain, w, pos, *, tm=1024, tn=1024):
    m, d = x.shape
    n = w.shape[1]
    half = ROPE_DIM // 2
    inv_freq = 1.0 / (ROPE_THETA ** (jnp.arange(half, dtype=F32) * (2.0 / ROPE_DIM)))
    zeros = jnp.zeros((HEAD_DIM - ROPE_DIM,), F32)
    freq = jnp.concatenate([inv_freq, inv_freq, zeros]).reshape(1, HEAD_DIM)
    sign = jnp.concatenate([-jnp.ones((half,), F32), jnp.ones((half,), F32), zeros]).reshape(1, HEAD_DIM)
    d_attn = n // 3
    kern = functools.partial(_qkv_kernel, q_tiles=d_attn // tn, qk_tiles=2 * d_attn // tn,
                             heads_per_tile=tn // HEAD_DIM)
    return pl.pallas_call(
        kern,
        grid=(m // tm, n // tn),
        in_specs=[pl.BlockSpec((tm, d), lambda i, j: (i, 0)),
                  pl.BlockSpec((1, d), lambda i, j: (0, 0)),
                  pl.BlockSpec((d, tn), lambda i, j: (0, j)),
                  pl.BlockSpec((tm, 1), lambda i, j: (i, 0)),
                  pl.BlockSpec((1, HEAD_DIM), lambda i, j: (0, 0)),
                  pl.BlockSpec((1, HEAD_DIM), lambda i, j: (0, 0))],
        out_specs=pl.BlockSpec((tm, tn), lambda i, j: (i, j)),
        out_shape=jax.ShapeDtypeStruct((m, n), F32),
        scratch_shapes=[pltpu.VMEM((tm, d), BF16), pltpu.VMEM((2, tm, HEAD_DIM), F32),
                        pltpu.VMEM((2, tm, HEAD_DIM), F32)],
        compiler_params=_params("parallel", "arbitrary"),
        name="qkv_rope",
    )(x, gain.reshape(1, d), w, pos.astype(F32).reshape(m, 1), freq, sign)


def _out_proj_kernel(*refs, n_parts):
    h_ref = refs[0]
    a_refs = refs[1:1 + n_parts]
    w_refs = refs[1 + n_parts:1 + 2 * n_parts]
    o_ref = refs[1 + 2 * n_parts]
    acc = h_ref[...]
    for a_ref, w_ref in zip(a_refs, w_refs):
        acc = acc + _dot(a_ref[...].astype(BF16), w_ref[...])
    o_ref[...] = acc


def _out_proj(h, parts, w, *, tm=512):
    m, n = h.shape
    k_part = parts[0].shape[1]
    assert all(a.shape[1] == k_part for a in parts) and k_part * len(parts) == w.shape[0]
    row = pl.BlockSpec((tm, n), lambda i: (i, 0))
    in_specs = [row]
    in_specs += [pl.BlockSpec((tm, k_part), lambda i: (i, 0)) for _ in parts]
    in_specs += [pl.BlockSpec((k_part, n), lambda i, p=p: (p, 0)) for p in range(len(parts))]
    return pl.pallas_call(
        functools.partial(_out_proj_kernel, n_parts=len(parts)),
        grid=(m // tm,),
        in_specs=in_specs,
        out_specs=row,
        out_shape=jax.ShapeDtypeStruct((m, n), F32),
        compiler_params=_params("parallel"),
        name="out_proj",
    )(h, *parts, *([w] * len(parts)))


def _mlp_kernel(h_ref, g_ref, w1_ref, w2_ref, fg_ref, o_ref, xn_ref, *, final_norm):
    k = pl.program_id(1)
    last = pl.num_programs(1) - 1
    row_chunks = [pl.ds(r, NORM_ROW_CHUNK) for r in range(0, h_ref.shape[0], NORM_ROW_CHUNK)]

    def contribution(xn):
        a = jnp.maximum(_dot(xn, w1_ref[...]), 0.0)
        return _dot((a * a).astype(BF16), w2_ref[...])

    @pl.when(k == 0)
    def _():
        for rows in row_chunks:
            x = h_ref[rows, :]
            xn = _rms_norm(x, g_ref[...]).astype(BF16)
            xn_ref[rows, :] = xn
            o_ref[rows, :] = x + contribution(xn)

    @pl.when((k > 0) & (k < last) if final_norm else k > 0)
    def _():
        o_ref[...] += contribution(xn_ref[...])

    if final_norm:
        @pl.when(k == last)
        def _():
            for rows in row_chunks:
                acc = o_ref[rows, :] + contribution(xn_ref[rows, :])
                o_ref[rows, :] = _rms_norm(acc, fg_ref[...])


def _mlp(h, gain, w1, w2, layer, final_gain=None, *, tm=1024, tf=512):
    m, d = h.shape
    f = w1.shape[2]
    fg = jnp.ones((d,), F32) if final_gain is None else final_gain
    return pl.pallas_call(
        functools.partial(_mlp_kernel, final_norm=final_gain is not None),
        grid=(m // tm, f // tf),
        in_specs=[pl.BlockSpec((tm, d), lambda i, k: (i, 0)),
                  pl.BlockSpec((1, d), lambda i, k: (0, 0)),
                  pl.BlockSpec((None, d, tf), lambda i, k: (layer, 0, k)),
                  pl.BlockSpec((None, tf, d), lambda i, k: (layer, k, 0)),
                  pl.BlockSpec((1, d), lambda i, k: (0, 0))],
        out_specs=pl.BlockSpec((tm, d), lambda i, k: (i, 0)),
        out_shape=jax.ShapeDtypeStruct((m, d), F32),
        scratch_shapes=[pltpu.VMEM((tm, d), BF16)],
        compiler_params=_params("parallel", "arbitrary"),
        name="mlp",
    )(h, gain.reshape(1, d), w1, w2, fg.reshape(1, d))


SUBLANES = 8


def _linear_scan(a, u, carry):
    rows = a.shape[0]
    sub = lax.broadcasted_iota(jnp.int32, a.shape, 0) & (SUBLANES - 1)
    d = 1
    while d < SUBLANES:
        keep = sub >= d
        u = u + a * jnp.where(keep, pltpu.roll(u, d, axis=0), 0.0)
        a = a * jnp.where(keep, pltpu.roll(a, d, axis=0), 1.0)
        d *= 2
    out = []
    for g in range(rows // SUBLANES):
        rows_g = slice(g * SUBLANES, (g + 1) * SUBLANES)
        h = u[rows_g] + a[rows_g] * carry
        carry = h[SUBLANES - 1:SUBLANES]
        out.append(h)
    return jnp.concatenate(out, axis=0), carry


def _lru_kernel(x_ref, y_ref, cw_ref, cb_ref, wa_ref, ba_ref, wi_ref, bi_ref, lam_ref, o_ref,
                xbuf_ref, carry_ref, *, tt):
    t = pl.program_id(1)

    @pl.when(t == 0)
    def _():
        xbuf_ref[0:CONV_HALO, :] = jnp.zeros((CONV_HALO, xbuf_ref.shape[1]), F32)
        carry_ref[...] = jnp.zeros(carry_ref.shape, F32)

    @pl.when(t > 0)
    def _():
        xbuf_ref[0:CONV_HALO, :] = xbuf_ref[tt:tt + CONV_HALO, :]

    xbuf_ref[CONV_HALO:, :] = x_ref[...]

    blk = xbuf_ref.shape[1] // LRU_HEADS
    for hh in range(LRU_HEADS):
        cols = slice(hh * blk, (hh + 1) * blk)
        xc = cb_ref[:, cols] + cw_ref[0:1, cols] * xbuf_ref[pl.ds(CONV_HALO - CONV_WIDTH + 1, tt), cols]
        for j in range(1, CONV_WIDTH):
            xc = xc + cw_ref[j:j + 1, cols] * xbuf_ref[pl.ds(CONV_HALO - CONV_WIDTH + 1 + j, tt), cols]

        xh = xc.astype(BF16)
        r = jax.nn.sigmoid(_dot(xh, wa_ref[hh]) + ba_ref[:, cols])
        gi = jax.nn.sigmoid(_dot(xh, wi_ref[hh]) + bi_ref[:, cols])

        neg_lam = -lam_ref[:, cols]
        softplus = jnp.maximum(neg_lam, 0.0) + jnp.log1p(jnp.exp(-jnp.abs(neg_lam)))
        log_a = (-LRU_C) * r * softplus
        a = jnp.exp(log_a)
        u = jnp.sqrt(-jnp.tanh(log_a) * (a * a + 1.0)) * (gi * xc)

        h, carry = _linear_scan(a, u, carry_ref[:, cols])
        carry_ref[:, cols] = carry

        y = y_ref[:, cols]
        gelu = 0.5 * y * (1.0 + jnp.tanh(math.sqrt(2.0 / math.pi) * (y + 0.044715 * (y * y * y))))
        o_ref[:, cols] = (h * gelu).astype(o_ref.dtype)


def _lru(proj, conv_w, conv_b, w_a, b_a, w_i, b_i, lam, *, batch, seq, tt=256):
    width = conv_w.shape[1]
    nt = seq // tt
    row = lambda b, t: (b * nt + t, 0)
    vec = pl.BlockSpec((1, width), lambda b, t: (0, 0))
    gate_w = pl.BlockSpec(w_a.shape, lambda b, t: (0, 0, 0))
    return pl.pallas_call(
        functools.partial(_lru_kernel, tt=tt),
        grid=(batch, nt),
        in_specs=[pl.BlockSpec((tt, width), row),
                  pl.BlockSpec((tt, width), lambda b, t: (b * nt + t, 1)),
                  pl.BlockSpec((CONV_WIDTH, width), lambda b, t: (0, 0)), vec,
                  gate_w, vec, gate_w, vec, vec],
        out_specs=pl.BlockSpec((tt, width), row),
        out_shape=jax.ShapeDtypeStruct((batch * seq, width), BF16),
        scratch_shapes=[pltpu.VMEM((tt + CONV_HALO, width), F32), pltpu.VMEM((1, width), F32)],
        compiler_params=_params("parallel", "arbitrary"),
        name="rg_lru",
    )(proj, proj, conv_w, conv_b.reshape(1, width), w_a, b_a.reshape(1, width),
      w_i, b_i.reshape(1, width), lam.reshape(1, width))


def _cumsum_rows(x):
    rows = x.shape[0]
    row = lax.broadcasted_iota(jnp.int32, x.shape, 0)
    d = 1
    while d < rows:
        x = x + jnp.where(row >= d, pltpu.roll(x, d, axis=0), 0.0)
        d *= 2
    return x


def _hgrn_kernel(q_ref, f_ref, v_ref, g_ref, lbl_ref, gn_ref, o_ref, state_ref, level_ref,
                 *, layer, chunk, n_chunks):
    ti = lax.broadcasted_iota(jnp.int32, (chunk, chunk), 0)
    si = lax.broadcasted_iota(jnp.int32, (chunk, chunk), 1)
    tx = ti ^ si

    @pl.when(pl.program_id(2) == 0)
    def _():
        state_ref[...] = jnp.zeros(state_ref.shape, F32)
        for li in range(level_ref.shape[0]):
            s = 1 << li
            level_ref[li] = jnp.where((tx >= s) & (tx < 2 * s) & (ti > si), 1.0, 0.0)

    logits = lbl_ref[...]
    e = jnp.exp(logits - jnp.max(logits, axis=0, keepdims=True))
    lb = jnp.sum(e[0:layer + 1, :], axis=0, keepdims=True) / jnp.sum(e, axis=0, keepdims=True)

    row = lax.broadcasted_iota(jnp.int32, (chunk, HGRN_HEAD), 0)

    for c in range(n_chunks):
        rows = pl.ds(c * chunk, chunk)
        qr = q_ref[rows, :]
        fz = f_ref[rows, :]
        v = v_ref[rows, :].astype(BF16)
        q = qr * jax.nn.sigmoid(qr)
        log_f = jnp.log(lb + (1.0 - lb) * jax.nn.sigmoid(fz))
        kk = (1.0 - lb) * jax.nn.sigmoid(-fz)
        b = _cumsum_rows(log_f)

        scores = jnp.where(tx == 0, jnp.sum(q * kk, axis=-1, keepdims=True), 0.0)
        b_end = b
        s = 1
        for li in range(level_ref.shape[0]):
            upper = (row & s) != 0
            decay = jnp.exp(jnp.where(upper, b - pltpu.roll(b_end, s, axis=0), b_end - b))
            z = (jnp.where(upper, q, kk) * decay).astype(BF16)
            scores = scores + _dot_nt(z, z) * level_ref[li]
            b_end = jnp.where(upper, b_end, pltpu.roll(b_end, chunk - s, axis=0))
            s *= 2

        state = state_ref[...]
        o = _dot(scores.astype(BF16), v) + _dot_nt((q * jnp.exp(b)).astype(BF16), state.astype(BF16))
        kd = (kk * jnp.exp(b_end - b)).astype(BF16)
        state_ref[...] = state * jnp.exp(b_end[0:1, :]) + _dot_tn(v, kd)

        o = o * lax.rsqrt(jnp.mean(o * o, axis=-1, keepdims=True) + NORM_EPS) * gn_ref[...]
        gr = g_ref[rows, :]
        o_ref[rows, :] = (o * (gr * jax.nn.sigmoid(gr))).astype(o_ref.dtype)


def _hgrn(proj, lb_logits, g_norm, *, batch, seq, layer, col0, tt=512, chunk=128):
    width = g_norm.shape[0]
    heads = width // HGRN_HEAD
    nt = seq // tt
    hb = width // HGRN_HEAD
    c0 = col0 // HGRN_HEAD

    def col(group):
        return pl.BlockSpec((tt, HGRN_HEAD), lambda b, h, t: (b * nt + t, c0 + group * hb + h))

    return pl.pallas_call(
        functools.partial(_hgrn_kernel, layer=layer, chunk=chunk, n_chunks=tt // chunk),
        grid=(batch, heads, nt),
        in_specs=[col(0), col(1), col(2), col(3),
                  pl.BlockSpec((lb_logits.shape[0], HGRN_HEAD), lambda b, h, t: (0, h)),
                  pl.BlockSpec((1, HGRN_HEAD), lambda b, h, t: (0, h))],
        out_specs=pl.BlockSpec((tt, HGRN_HEAD), lambda b, h, t: (b * nt + t, h)),
        out_shape=jax.ShapeDtypeStruct((batch * seq, width), BF16),
        scratch_shapes=[pltpu.VMEM((HGRN_HEAD, HGRN_HEAD), F32),
                        pltpu.VMEM((chunk.bit_length() - 1, chunk, chunk), F32)],
        compiler_params=_params("parallel", "parallel", "arbitrary"),
        name="hgrn2",
    )(proj, proj, proj, proj, lb_logits, g_norm.reshape(1, width))


def _attn_kernel(q_ref, k_ref, v_ref, o_ref, qc_ref, kbuf_ref, vbuf_ref, num_ref, max_ref, den_ref, bias_ref,
                 *, unroll):
    t = pl.program_id(2)
    blk = ATTN_BLOCK
    res = ATTN_RES

    @pl.when(t == 0)
    def _():
        kbuf_ref[:, 0:blk, :] = jnp.zeros((res, blk, HEAD_DIM), F32)
        vbuf_ref[:, 0:blk, :] = jnp.zeros((res, blk, HEAD_DIM), F32)
        a = lax.broadcasted_iota(jnp.int32, (blk, 2 * blk), 0)
        c = lax.broadcasted_iota(jnp.int32, (blk, 2 * blk), 1)
        for bi, dil in enumerate(DILATIONS):
            runs = res // dil
            q_run = blk // runs
            k_run = 2 * blk // runs
            dist = blk + runs * (a % q_run - c % k_run) + (a // q_run - c // k_run)
            ok = (dist >= 0) & (dist <= blk)
            bias_ref[2 * bi] = jnp.where(ok, 0.0, -jnp.inf)
            bias_ref[2 * bi + 1] = jnp.where(ok & (c % k_run >= q_run), 0.0, -jnp.inf)

    @pl.when(t > 0)
    def _():
        kbuf_ref[:, 0:blk, :] = kbuf_ref[:, blk:2 * blk, :]
        vbuf_ref[:, 0:blk, :] = vbuf_ref[:, blk:2 * blk, :]

    for r in range(res):
        rows = pl.ds(r, blk, stride=res)
        qc_ref[r] = q_ref[rows, :]
        kbuf_ref[r, blk:2 * blk, :] = k_ref[rows, :]
        vbuf_ref[r, blk:2 * blk, :] = v_ref[rows, :]

    ones = jnp.ones((2 * blk, HEAD_DIM), BF16)
    for bi, dil in enumerate(DILATIONS):
        runs = res // dil
        q_run = blk // runs
        k_run = 2 * blk // runs

        def block(idx, bi=bi, dil=dil, runs=runs, q_run=q_run, k_run=k_run):
            r_d = idx // runs
            n = idx % runs
            q0 = pl.multiple_of(n * q_run, q_run)
            k0 = pl.multiple_of(blk + (n - 1) * q_run, q_run)
            planes = [r_d + dil * j for j in range(runs)]
            qb = jnp.concatenate([qc_ref[p, pl.ds(q0, q_run), :] for p in planes], axis=0).astype(BF16)
            kb = jnp.concatenate([kbuf_ref[p, pl.ds(k0, k_run), :] for p in planes], axis=0).astype(BF16)
            vb = jnp.concatenate([vbuf_ref[p, pl.ds(k0, k_run), :] for p in planes], axis=0).astype(BF16)
            first = ((t == 0) & (n == 0)).astype(jnp.int32)
            s = _dot_nt(qb, kb) + bias_ref[2 * bi + first]
            m = jnp.max(s, axis=-1, keepdims=True)
            p = jnp.exp2(s - m).astype(BF16)
            ol = _dot(p, jnp.concatenate([vb, ones], axis=1))
            o = ol[:, :HEAD_DIM]
            l = ol[:, HEAD_DIM:]
            for j, plane in enumerate(planes):
                run = slice(j * q_run, (j + 1) * q_run)
                dst = (plane, pl.ds(q0, q_run), slice(None))
                m_j = jnp.broadcast_to(m[run], (q_run, HEAD_DIM))
                if bi == 0:
                    max_ref[dst] = m_j
                    den_ref[dst] = l[run]
                    num_ref[dst] = o[run]
                else:
                    m_old = max_ref[dst]
                    m_new = jnp.maximum(m_old, m_j)
                    w_old = jnp.exp2(m_old - m_new)
                    w_new = jnp.exp2(m_j - m_new)
                    max_ref[dst] = m_new
                    den_ref[dst] = den_ref[dst] * w_old + l[run] * w_new
                    num_ref[dst] = num_ref[dst] * w_old + o[run] * w_new

        def body(i, carry, block=block):
            for u in range(unroll):
                block(i * unroll + u)
            return carry

        lax.fori_loop(0, res // unroll, body, 0)

    for r in range(res):
        o_ref[pl.ds(r, blk, stride=res), :] = num_ref[r] / den_ref[r]


def _attention(qkv, *, batch, seq, unroll=8):
    tq = ATTN_BLOCK * ATTN_RES
    nt = seq // tq
    d_attn = ATTN_HEADS * HEAD_DIM

    def col(group):
        return pl.BlockSpec((tq, HEAD_DIM), lambda b, h, t: (b * nt + t, group * ATTN_HEADS + h))

    plane = pltpu.VMEM((ATTN_RES, ATTN_BLOCK, HEAD_DIM), F32)
    band = pltpu.VMEM((ATTN_RES, 2 * ATTN_BLOCK, HEAD_DIM), F32)
    return pl.pallas_call(
        functools.partial(_attn_kernel, unroll=unroll),
        grid=(batch, ATTN_HEADS, nt),
        in_specs=[col(0), col(1), col(2)],
        out_specs=pl.BlockSpec((tq, HEAD_DIM), lambda b, h, t: (b * nt + t, h)),
        out_shape=jax.ShapeDtypeStruct((batch * seq, d_attn), F32),
        scratch_shapes=[plane, band, band, plane, plane, plane,
                        pltpu.VMEM((2 * len(DILATIONS), ATTN_BLOCK, 2 * ATTN_BLOCK), F32)],
        compiler_params=_params("parallel", "parallel", "arbitrary"),
        name="dilated_attention",
    )(qkv, qkv, qkv)


def kernel(x, positions, norm_mix, norm_mlp, final_norm, rec_w_in, rec_conv_w, rec_conv_b, lru_w_a, lru_b_a, lru_w_i, lru_b_i, lru_lambda, hgrn_lb_logits, hgrn_g_norm, rec_w_out, attn_w_qkv, attn_w_o, mlp_w1, mlp_w2):
    batch, seq, d = x.shape
    depth = norm_mix.shape[0]
    lru_width = rec_conv_w.shape[2]
    assert seq % (ATTN_BLOCK * ATTN_RES) == 0 and DILATIONS[-1] == ATTN_RES
    h = x.reshape(batch * seq, d)
    pos = positions.reshape(batch * seq)
    mlp_w1 = mlp_w1.astype(BF16)
    mlp_w2 = mlp_w2.astype(BF16)
    for layer in range(depth):
        j = layer // 2
        if layer % 2 == 0:
            proj = _norm_matmul(h, norm_mix[layer], rec_w_in[j].astype(BF16))
            lru = _lru(proj, rec_conv_w[j], rec_conv_b[j], lru_w_a[j].astype(BF16), lru_b_a[j],
                       lru_w_i[j].astype(BF16), lru_b_i[j], lru_lambda[j], batch=batch, seq=seq)
            hg = _hgrn(proj, hgrn_lb_logits, hgrn_g_norm[j], batch=batch, seq=seq, layer=layer,
                       col0=2 * lru_width)
            h = _out_proj(h, [lru, hg], rec_w_out[j].astype(BF16))
        else:
            qkv = _qkv_proj(h, norm_mix[layer], attn_w_qkv[j].astype(BF16), pos)
            attn = _attention(qkv, batch=batch, seq=seq)
            h = _out_proj(h, [attn], attn_w_o[j].astype(BF16))
        last = layer == depth - 1
        h = _mlp(h, norm_mlp[layer], mlp_w1, mlp_w2, layer, final_norm if last else None)
    if depth == 0:
        h = _rms_norm(h, final_norm)
    return h.reshape(batch, seq, d)
```

```python
import functools
import math

import jax
import jax.numpy as jnp
from jax import lax
from jax.experimental import pallas as pl
from jax.experimental.pallas import tpu as pltpu

F32 = jnp.float32
BF16 = jnp.bfloat16

NORM_EPS = 1e-6
LRU_C = 8.0
LRU_HEADS = 4
CONV_WIDTH = 4
HGRN_HEAD = 128
ATTN_HEADS = 16
HEAD_DIM = 128
ROPE_DIM = 32
ROPE_THETA = 500000.0
DILATIONS = (1, 4, 16)
ATTN_BLOCK = 128
ATTN_RES = 16
LANES = 128
MXU_COLS = 256
NORM_ROW_CHUNK = 256
LRU_ROW_CHUNK = 128
QKV_CHUNK_HEADS = 2
CONV_HALO = 8

VMEM_LIMIT = 56 * 1024 * 1024


def _params(*sem):
    return pltpu.CompilerParams(dimension_semantics=sem, vmem_limit_bytes=VMEM_LIMIT)


def _rms_norm(x, gain):
    return x * lax.rsqrt(jnp.mean(x * x, axis=-1, keepdims=True) + NORM_EPS) * gain


def _dot(a, b):
    return jnp.dot(a, b, preferred_element_type=F32)


def _dot_nt(a, b):
    return lax.dot_general(a, b, (((1,), (1,)), ((), ())), preferred_element_type=F32)


def _dot_tn(a, b):
    return lax.dot_general(a, b, (((0,), (0,)), ((), ())), preferred_element_type=F32)


def _qkv_kernel(x_ref, g_ref, w_ref, pos_ref, freq_ref, sign_ref, o_ref, xn_ref, cos_ref, sin_ref,
                *, q_tiles, qk_tiles, heads_per_tile):
    j = pl.program_id(1)
    half = ROPE_DIM // 2

    def rotated_tile(rows, xn, cos, sin):
        lane = lax.broadcasted_iota(jnp.int32, cos.shape, 1)
        for c in range(heads_per_tile // QKV_CHUNK_HEADS):
            c0 = c * QKV_CHUNK_HEADS * HEAD_DIM
            acc = _dot(xn, w_ref[:, c0:c0 + QKV_CHUNK_HEADS * HEAD_DIM])
            for hh in range(QKV_CHUNK_HEADS):
                t = acc[:, hh * HEAD_DIM:(hh + 1) * HEAD_DIM]
                partner = jnp.where(lane < half, pltpu.roll(t, HEAD_DIM - half, axis=1),
                                    pltpu.roll(t, half, axis=1))
                o_ref[rows, c0 + hh * HEAD_DIM:c0 + (hh + 1) * HEAD_DIM] = t * cos + partner * sin

    @pl.when(j == 0)
    def _():
        for r in range(0, x_ref.shape[0], NORM_ROW_CHUNK):
            rows = pl.ds(r, NORM_ROW_CHUNK)
            xn = _rms_norm(x_ref[rows, :], g_ref[...]).astype(BF16)
            xn_ref[rows, :] = xn
            ang = pos_ref[rows, :] * freq_ref[...]
            cos = jnp.cos(ang)
            sin = jnp.sin(ang) * sign_ref[...]
            scale = F32(HEAD_DIM ** -0.5 * math.log2(math.e))
            cos_ref[0, rows, :] = cos * scale
            sin_ref[0, rows, :] = sin * scale
            cos_ref[1, rows, :] = cos
            sin_ref[1, rows, :] = sin
            rotated_tile(rows, xn, cos * scale, sin * scale)

    @pl.when((j > 0) & (j < qk_tiles))
    def _():
        group = (j >= q_tiles).astype(jnp.int32)
        rotated_tile(slice(None), xn_ref[...], cos_ref[group], sin_ref[group])

    @pl.when(j >= qk_tiles)
    def _():
        o_ref[...] = _dot(xn_ref[...], w_ref[...])


def _qkv_proj(x, gain, w, pos, *, tm=1024, tn=1024):
    m, d = x.shape
    n = w.shape[1]
    half = ROPE_DIM // 2
    inv_freq = 1.0 / (ROPE_THETA ** (jnp.arange(half, dtype=F32) * (2.0 / ROPE_DIM)))
    zeros = jnp.zeros((HEAD_DIM - ROPE_DIM,), F32)
    freq = jnp.concatenate([inv_freq, inv_freq, zeros]).reshape(1, HEAD_DIM)
    sign = jnp.concatenate([-jnp.ones((half,), F32), jnp.ones((half,), F32), zeros]).reshape(1, HEAD_DIM)
    d_attn = n // 3
    kern = functools.partial(_qkv_kernel, q_tiles=d_attn // tn, qk_tiles=2 * d_attn // tn,
                             heads_per_tile=tn // HEAD_DIM)
    return pl.pallas_call(
        kern,
        grid=(m // tm, n // tn),
        in_specs=[pl.BlockSpec((tm, d), lambda i, j: (i, 0)),
                  pl.BlockSpec((1, d), lambda i, j: (0, 0)),
                  pl.BlockSpec((d, tn), lambda i, j: (0, j)),
                  pl.BlockSpec((tm, 1), lambda i, j: (i, 0)),
                  pl.BlockSpec((1, HEAD_DIM), lambda i, j: (0, 0)),
                  pl.BlockSpec((1, HEAD_DIM), lambda i, j: (0, 0))],
        out_specs=pl.BlockSpec((tm, tn), lambda i, j: (i, j)),
        out_shape=jax.ShapeDtypeStruct((m, n), F32),
        scratch_shapes=[pltpu.VMEM((tm, d), BF16), pltpu.VMEM((2, tm, HEAD_DIM), F32),
                        pltpu.VMEM((2, tm, HEAD_DIM), F32)],
        compiler_params=_params("parallel", "arbitrary"),
        name="qkv_rope",
    )(x, gain.reshape(1, d), w, pos.astype(F32).reshape(m, 1), freq, sign)


def _out_proj_kernel(*refs, n_parts):
    h_ref = refs[0]
    a_refs = refs[1:1 + n_parts]
    w_refs = refs[1 + n_parts:1 + 2 * n_parts]
    o_ref = refs[1 + 2 * n_parts]
    acc = h_ref[...]
    for a_ref, w_ref in zip(a_refs, w_refs):
        acc = acc + _dot(a_ref[...].astype(BF16), w_ref[...])
    o_ref[...] = acc


def _out_proj(h, parts, w, *, tm=512):
    m, n = h.shape
    k_part = parts[0].shape[1]
    assert all(a.shape[1] == k_part for a in parts) and k_part * len(parts) == w.shape[0]
    row = pl.BlockSpec((tm, n), lambda i: (i, 0))
    in_specs = [row]
    in_specs += [pl.BlockSpec((tm, k_part), lambda i: (i, 0)) for _ in parts]
    in_specs += [pl.BlockSpec((k_part, n), lambda i, p=p: (p, 0)) for p in range(len(parts))]
    return pl.pallas_call(
        functools.partial(_out_proj_kernel, n_parts=len(parts)),
        grid=(m // tm,),
        in_specs=in_specs,
        out_specs=row,
        out_shape=jax.ShapeDtypeStruct((m, n), F32),
        compiler_params=_params("parallel"),
        name="out_proj",
    )(h, *parts, *([w] * len(parts)))


def _mlp_kernel(h_ref, g_ref, w1_ref, w2_ref, fg_ref, o_ref, xn_ref, *, final_norm):
    k = pl.program_id(1)
    last = pl.num_programs(1) - 1
    row_chunks = [pl.ds(r, NORM_ROW_CHUNK) for r in range(0, h_ref.shape[0], NORM_ROW_CHUNK)]

    def contribution(xn):
        a = jnp.maximum(_dot(xn, w1_ref[...]), 0.0)
        return _dot((a * a).astype(BF16), w2_ref[...])

    @pl.when(k == 0)
    def _():
        for rows in row_chunks:
            x = h_ref[rows, :]
            xn = _rms_norm(x, g_ref[...]).astype(BF16)
            xn_ref[rows, :] = xn
            o_ref[rows, :] = x + contribution(xn)

    @pl.when((k > 0) & (k < last) if final_norm else k > 0)
    def _():
        o_ref[...] += contribution(xn_ref[...])

    if final_norm:
        @pl.when(k == last)
        def _():
            for rows in row_chunks:
                acc = o_ref[rows, :] + contribution(xn_ref[rows, :])
                o_ref[rows, :] = _rms_norm(acc, fg_ref[...])


def _mlp(h, gain, w1, w2, layer, final_gain=None, *, tm=1024, tf=512):
    m, d = h.shape
    f = w1.shape[2]
    fg = jnp.ones((d,), F32) if final_gain is None else final_gain
    return pl.pallas_call(
        functools.partial(_mlp_kernel, final_norm=final_gain is not None),
        grid=(m // tm, f // tf),
        in_specs=[pl.BlockSpec((tm, d), lambda i, k: (i, 0)),
                  pl.BlockSpec((1, d), lambda i, k: (0, 0)),
                  pl.BlockSpec((None, d, tf), lambda i, k: (layer, 0, k)),
                  pl.BlockSpec((None, tf, d), lambda i, k: (layer, k, 0)),
                  pl.BlockSpec((1, d), lambda i, k: (0, 0))],
        out_specs=pl.BlockSpec((tm, d), lambda i, k: (i, 0)),
        out_shape=jax.ShapeDtypeStruct((m, d), F32),
        scratch_shapes=[pltpu.VMEM((tm, d), BF16)],
        compiler_params=_params("parallel", "arbitrary"),
        name="mlp",
    )(h, gain.reshape(1, d), w1, w2, fg.reshape(1, d))


SUBLANES = 8


def _linear_scan(a, u, carry):
    rows = a.shape[0]
    sub = lax.broadcasted_iota(jnp.int32, a.shape, 0) & (SUBLANES - 1)
    d = 1
    while d < SUBLANES:
        keep = sub >= d
        u = u + a * jnp.where(keep, pltpu.roll(u, d, axis=0), 0.0)
        a = a * jnp.where(keep, pltpu.roll(a, d, axis=0), 1.0)
        d *= 2
    out = []
    for g in range(rows // SUBLANES):
        rows_g = slice(g * SUBLANES, (g + 1) * SUBLANES)
        h = u[rows_g] + a[rows_g] * carry
        carry = h[SUBLANES - 1:SUBLANES]
        out.append(h)
    return jnp.concatenate(out, axis=0), carry


def _lru_rows(hh, r0, n_rows, xbuf_ref, ybuf_ref, cw_ref, cb_ref, wa_ref, ba_ref, wi_ref, bi_ref, lam_ref,
              carry):
    blk = ybuf_ref.shape[1] // LRU_HEADS
    cols = slice(hh * blk, (hh + 1) * blk)
    tap0 = CONV_HALO - CONV_WIDTH + 1 + r0
    xc = cb_ref[:, cols] + cw_ref[0:1, cols] * xbuf_ref[pl.ds(tap0, n_rows), cols]
    for j in range(1, CONV_WIDTH):
        xc = xc + cw_ref[j:j + 1, cols] * xbuf_ref[pl.ds(tap0 + j, n_rows), cols]

    xh = xc.astype(BF16)
    r = jax.nn.sigmoid(_dot(xh, wa_ref[hh]) + ba_ref[:, cols])
    gi = jax.nn.sigmoid(_dot(xh, wi_ref[hh]) + bi_ref[:, cols])

    neg_lam = -lam_ref[:, cols]
    softplus = jnp.maximum(neg_lam, 0.0) + jnp.log1p(jnp.exp(-jnp.abs(neg_lam)))
    log_a = (-LRU_C) * r * softplus
    a = jnp.exp(log_a)
    u = jnp.sqrt(-jnp.tanh(log_a) * (a * a + 1.0)) * (gi * xc)

    h, carry = _linear_scan(a, u, carry)

    y = ybuf_ref[pl.ds(r0, n_rows), cols]
    gelu = 0.5 * y * (1.0 + jnp.tanh(math.sqrt(2.0 / math.pi) * (y + 0.044715 * (y * y * y))))
    return (h * gelu).astype(BF16), carry


def _in_proj_lru_kernel(x_ref, g_ref, w_ref, cw_ref, cb_ref, wa_ref, ba_ref, wi_ref, bi_ref, lam_ref,
                        proj_ref, lru_ref, xn_ref, xbuf_ref, ybuf_ref, carry_ref, *, tiles_per_seq):
    i = pl.program_id(0)
    j = pl.program_id(1)
    tm = x_ref.shape[0]

    @pl.when(j == 0)
    def _():
        @pl.when(i % tiles_per_seq == 0)
        def _():
            xbuf_ref[0:CONV_HALO, :] = jnp.zeros((CONV_HALO, xbuf_ref.shape[1]), F32)
            carry_ref[...] = jnp.zeros(carry_ref.shape, F32)

        @pl.when(i % tiles_per_seq != 0)
        def _():
            xbuf_ref[0:CONV_HALO, :] = xbuf_ref[tm:tm + CONV_HALO, :]

        for r in range(0, tm, NORM_ROW_CHUNK):
            rows = pl.ds(r, NORM_ROW_CHUNK)
            xn = _rms_norm(x_ref[rows, :], g_ref[...]).astype(BF16)
            xn_ref[rows, :] = xn
            xbuf_ref[pl.ds(CONV_HALO + r, NORM_ROW_CHUNK), :] = _dot(xn, w_ref[...])

    @pl.when(j == 1)
    def _():
        ybuf_ref[...] = _dot(xn_ref[...], w_ref[...])

    for hh in range(LRU_HEADS):
        @pl.when(j == 2 + hh)
        def _(hh=hh):
            blk = ybuf_ref.shape[1] // LRU_HEADS
            cols = slice(hh * blk, (hh + 1) * blk)
            carry = carry_ref[:, cols]
            n_col_chunks = w_ref.shape[1] // MXU_COLS
            rows_per_col_chunk = tm // n_col_chunks
            for c in range(n_col_chunks):
                c_cols = slice(c * MXU_COLS, (c + 1) * MXU_COLS)
                proj_ref[:, c_cols] = _dot(xn_ref[...], w_ref[:, c_cols])
                for r0 in range(c * rows_per_col_chunk, (c + 1) * rows_per_col_chunk, LRU_ROW_CHUNK):
                    out, carry = _lru_rows(hh, r0, LRU_ROW_CHUNK, xbuf_ref, ybuf_ref, cw_ref, cb_ref,
                                           wa_ref, ba_ref, wi_ref, bi_ref, lam_ref, carry)
                    lru_ref[pl.ds(r0, LRU_ROW_CHUNK), :] = out
            carry_ref[:, cols] = carry


def _in_proj_lru(x, gain, w, conv_w, conv_b, w_a, b_a, w_i, b_i, lam, *, seq, tm=1024):
    m, d = x.shape
    width = conv_w.shape[1]
    tn = width
    n_steps = w.shape[1] // tn
    assert n_steps == 2 + LRU_HEADS and seq % tm == 0
    blk = width // LRU_HEADS
    const2 = lambda i, j: (0, 0)
    vec = pl.BlockSpec((1, width), const2)
    gate_w = pl.BlockSpec(w_a.shape, lambda i, j: (0, 0, 0))
    late = lambda i, j: (i, jnp.maximum(j - 2, 0))
    return pl.pallas_call(
        functools.partial(_in_proj_lru_kernel, tiles_per_seq=seq // tm),
        grid=(m // tm, n_steps),
        in_specs=[pl.BlockSpec((tm, d), lambda i, j: (i, 0)),
                  pl.BlockSpec((1, d), const2),
                  pl.BlockSpec((d, tn), lambda i, j: (0, j)),
                  pl.BlockSpec((CONV_WIDTH, width), const2), vec,
                  gate_w, vec, gate_w, vec, vec],
        out_specs=[pl.BlockSpec((tm, tn), late), pl.BlockSpec((tm, blk), late)],
        out_shape=[jax.ShapeDtypeStruct((m, w.shape[1] - 2 * width), F32),
                   jax.ShapeDtypeStruct((m, width), BF16)],
        scratch_shapes=[pltpu.VMEM((tm, d), BF16), pltpu.VMEM((tm + CONV_HALO, width), F32),
                        pltpu.VMEM((tm, width), F32), pltpu.VMEM((1, width), F32)],
        compiler_params=_params("arbitrary", "arbitrary"),
        name="in_proj_rg_lru",
    )(x, gain.reshape(1, d), w, conv_w, conv_b.reshape(1, width), w_a, b_a.reshape(1, width),
      w_i, b_i.reshape(1, width), lam.reshape(1, width))


def _cumsum_rows(x):
    rows = x.shape[0]
    row = lax.broadcasted_iota(jnp.int32, x.shape, 0)
    d = 1
    while d < rows:
        x = x + jnp.where(row >= d, pltpu.roll(x, d, axis=0), 0.0)
        d *= 2
    return x


def _hgrn_kernel(q_ref, f_ref, v_ref, g_ref, lbl_ref, gn_ref, o_ref, state_ref, level_ref,
                 *, layer, chunk, n_chunks):
    ti = lax.broadcasted_iota(jnp.int32, (chunk, chunk), 0)
    si = lax.broadcasted_iota(jnp.int32, (chunk, chunk), 1)
    tx = ti ^ si

    @pl.when(pl.program_id(2) == 0)
    def _():
        state_ref[...] = jnp.zeros(state_ref.shape, F32)
        for li in range(level_ref.shape[0]):
            s = 1 << li
            level_ref[li] = jnp.where((tx >= s) & (tx < 2 * s) & (ti > si), 1.0, 0.0)

    logits = lbl_ref[...]
    e = jnp.exp(logits - jnp.max(logits, axis=0, keepdims=True))
    lb = jnp.sum(e[0:layer + 1, :], axis=0, keepdims=True) / jnp.sum(e, axis=0, keepdims=True)

    row = lax.broadcasted_iota(jnp.int32, (chunk, HGRN_HEAD), 0)

    for c in range(n_chunks):
        rows = pl.ds(c * chunk, chunk)
        qr = q_ref[rows, :]
        fz = f_ref[rows, :]
        v = v_ref[rows, :].astype(BF16)
        q = qr * jax.nn.sigmoid(qr)
        log_f = jnp.log(lb + (1.0 - lb) * jax.nn.sigmoid(fz))
        kk = (1.0 - lb) * jax.nn.sigmoid(-fz)
        b = _cumsum_rows(log_f)

        scores = jnp.where(tx == 0, jnp.sum(q * kk, axis=-1, keepdims=True), 0.0)
        b_end = b
        s = 1
        for li in range(level_ref.shape[0]):
            upper = (row & s) != 0
            decay = jnp.exp(jnp.where(upper, b - pltpu.roll(b_end, s, axis=0), b_end - b))
            z = (jnp.where(upper, q, kk) * decay).astype(BF16)
            scores = scores + _dot_nt(z, z) * level_ref[li]
            b_end = jnp.where(upper, b_end, pltpu.roll(b_end, chunk - s, axis=0))
            s *= 2

        state = state_ref[...]
        o = _dot(scores.astype(BF16), v) + _dot_nt((q * jnp.exp(b)).astype(BF16), state.astype(BF16))
        kd = (kk * jnp.exp(b_end - b)).astype(BF16)
        state_ref[...] = state * jnp.exp(b_end[0:1, :]) + _dot_tn(v, kd)

        o = o * lax.rsqrt(jnp.mean(o * o, axis=-1, keepdims=True) + NORM_EPS) * gn_ref[...]
        gr = g_ref[rows, :]
        o_ref[rows, :] = (o * (gr * jax.nn.sigmoid(gr))).astype(o_ref.dtype)


def _hgrn(proj, lb_logits, g_norm, *, batch, seq, layer, col0, tt=2048, chunk=128):
    width = g_norm.shape[0]
    heads = width // HGRN_HEAD
    nt = seq // tt
    hb = width // HGRN_HEAD
    c0 = col0 // HGRN_HEAD

    def col(group):
        return pl.BlockSpec((tt, HGRN_HEAD), lambda b, h, t: (b * nt + t, c0 + group * hb + h))

    return pl.pallas_call(
        functools.partial(_hgrn_kernel, layer=layer, chunk=chunk, n_chunks=tt // chunk),
        grid=(batch, heads, nt),
        in_specs=[col(0), col(1), col(2), col(3),
                  pl.BlockSpec((lb_logits.shape[0], HGRN_HEAD), lambda b, h, t: (0, h)),
                  pl.BlockSpec((1, HGRN_HEAD), lambda b, h, t: (0, h))],
        out_specs=pl.BlockSpec((tt, HGRN_HEAD), lambda b, h, t: (b * nt + t, h)),
        out_shape=jax.ShapeDtypeStruct((batch * seq, width), BF16),
        scratch_shapes=[pltpu.VMEM((HGRN_HEAD, HGRN_HEAD), F32),
                        pltpu.VMEM((chunk.bit_length() - 1, chunk, chunk), F32)],
        compiler_params=_params("parallel", "parallel", "arbitrary"),
        name="hgrn2",
    )(proj, proj, proj, proj, lb_logits, g_norm.reshape(1, width))


def _attn_kernel(q_ref, k_ref, v_ref, o_ref, qc_ref, kbuf_ref, vbuf_ref, num_ref, max_ref, den_ref, bias_ref,
                 *, unroll):
    t = pl.program_id(2)
    blk = ATTN_BLOCK
    res = ATTN_RES

    @pl.when(t == 0)
    def _():
        kbuf_ref[:, 0:blk, :] = jnp.zeros((res, blk, HEAD_DIM), F32)
        vbuf_ref[:, 0:blk, :] = jnp.zeros((res, blk, HEAD_DIM), F32)
        a = lax.broadcasted_iota(jnp.int32, (blk, 2 * blk), 0)
        c = lax.broadcasted_iota(jnp.int32, (blk, 2 * blk), 1)
        for bi, dil in enumerate(DILATIONS):
            runs = res // dil
            q_run = blk // runs
            k_run = 2 * blk // runs
            dist = blk + runs * (a % q_run - c % k_run) + (a // q_run - c // k_run)
            ok = (dist >= 0) & (dist <= blk)
            bias_ref[2 * bi] = jnp.where(ok, 0.0, -jnp.inf)
            bias_ref[2 * bi + 1] = jnp.where(ok & (c % k_run >= q_run), 0.0, -jnp.inf)

    @pl.when(t > 0)
    def _():
        kbuf_ref[:, 0:blk, :] = kbuf_ref[:, blk:2 * blk, :]
        vbuf_ref[:, 0:blk, :] = vbuf_ref[:, blk:2 * blk, :]

    for r in range(res):
        rows = pl.ds(r, blk, stride=res)
        qc_ref[r] = q_ref[rows, :]
        kbuf_ref[r, blk:2 * blk, :] = k_ref[rows, :]
        vbuf_ref[r, blk:2 * blk, :] = v_ref[rows, :]

    ones = jnp.ones((2 * blk, HEAD_DIM), BF16)
    for bi, dil in enumerate(DILATIONS):
        runs = res // dil
        q_run = blk // runs
        k_run = 2 * blk // runs

        def block(idx, bi=bi, dil=dil, runs=runs, q_run=q_run, k_run=k_run):
            r_d = idx // runs
            n = idx % runs
            q0 = pl.multiple_of(n * q_run, q_run)
            k0 = pl.multiple_of(blk + (n - 1) * q_run, q_run)
            planes = [r_d + dil * j for j in range(runs)]
            qb = jnp.concatenate([qc_ref[p, pl.ds(q0, q_run), :] for p in planes], axis=0).astype(BF16)
            kb = jnp.concatenate([kbuf_ref[p, pl.ds(k0, k_run), :] for p in planes], axis=0).astype(BF16)
            vb = jnp.concatenate([vbuf_ref[p, pl.ds(k0, k_run), :] for p in planes], axis=0).astype(BF16)
            first = ((t == 0) & (n == 0)).astype(jnp.int32)
            s = _dot_nt(qb, kb) + bias_ref[2 * bi + first]
            m = jnp.max(s, axis=-1, keepdims=True)
            p = jnp.exp2(s - m).astype(BF16)
            ol = _dot(p, jnp.concatenate([vb, ones], axis=1))
            o = ol[:, :HEAD_DIM]
            l = ol[:, HEAD_DIM:]
            for j, plane in enumerate(planes):
                run = slice(j * q_run, (j + 1) * q_run)
                dst = (plane, pl.ds(q0, q_run), slice(None))
                m_j = jnp.broadcast_to(m[run], (q_run, HEAD_DIM))
                if bi == 0:
                    max_ref[dst] = m_j
                    den_ref[dst] = l[run]
                    num_ref[dst] = o[run]
                else:
                    m_old = max_ref[dst]
                    m_new = jnp.maximum(m_old, m_j)
                    w_old = jnp.exp2(m_old - m_new)
                    w_new = jnp.exp2(m_j - m_new)
                    max_ref[dst] = m_new
                    den_ref[dst] = den_ref[dst] * w_old + l[run] * w_new
                    num_ref[dst] = num_ref[dst] * w_old + o[run] * w_new

        def body(i, carry, block=block):
            for u in range(unroll):
                block(i * unroll + u)
            return carry

        lax.fori_loop(0, res // unroll, body, 0)

    for r in range(res):
        o_ref[pl.ds(r, blk, stride=res), :] = num_ref[r] / den_ref[r]


def _attention(qkv, *, batch, seq, unroll=16):
    tq = ATTN_BLOCK * ATTN_RES
    nt = seq // tq
    d_attn = ATTN_HEADS * HEAD_DIM

    def col(group):
        return pl.BlockSpec((tq, HEAD_DIM), lambda b, h, t: (b * nt + t, group * ATTN_HEADS + h))

    plane = pltpu.VMEM((ATTN_RES, ATTN_BLOCK, HEAD_DIM), F32)
    band = pltpu.VMEM((ATTN_RES, 2 * ATTN_BLOCK, HEAD_DIM), F32)
    return pl.pallas_call(
        functools.partial(_attn_kernel, unroll=unroll),
        grid=(batch, ATTN_HEADS, nt),
        in_specs=[col(0), col(1), col(2)],
        out_specs=pl.BlockSpec((tq, HEAD_DIM), lambda b, h, t: (b * nt + t, h)),
        out_shape=jax.ShapeDtypeStruct((batch * seq, d_attn), F32),
        scratch_shapes=[plane, band, band, plane, plane, plane,
                        pltpu.VMEM((2 * len(DILATIONS), ATTN_BLOCK, 2 * ATTN_BLOCK), F32)],
        compiler_params=_params("parallel", "parallel", "arbitrary"),
        name="dilated_attention",
    )(qkv, qkv, qkv)


def kernel(x, positions, norm_mix, norm_mlp, final_norm, rec_w_in, rec_conv_w, rec_conv_b, lru_w_a, lru_b_a, lru_w_i, lru_b_i, lru_lambda, hgrn_lb_logits, hgrn_g_norm, rec_w_out, attn_w_qkv, attn_w_o, mlp_w1, mlp_w2):
    batch, seq, d = x.shape
    depth = norm_mix.shape[0]
    assert seq % (ATTN_BLOCK * ATTN_RES) == 0 and DILATIONS[-1] == ATTN_RES
    h = x.reshape(batch * seq, d)
    pos = positions.reshape(batch * seq)
    mlp_w1 = mlp_w1.astype(BF16)
    mlp_w2 = mlp_w2.astype(BF16)
    for layer in range(depth):
        j = layer // 2
        if layer % 2 == 0:
            proj, lru = _in_proj_lru(h, norm_mix[layer], rec_w_in[j].astype(BF16), rec_conv_w[j],
                                     rec_conv_b[j], lru_w_a[j].astype(BF16), lru_b_a[j],
                                     lru_w_i[j].astype(BF16), lru_b_i[j], lru_lambda[j], seq=seq)
            hg = _hgrn(proj, hgrn_lb_logits, hgrn_g_norm[j], batch=batch, seq=seq, layer=layer, col0=0)
            h = _out_proj(h, [lru, hg], rec_w_out[j].astype(BF16))
        else:
            qkv = _qkv_proj(h, norm_mix[layer], attn_w_qkv[j].astype(BF16), pos)
            attn = _attention(qkv, batch=batch, seq=seq)
            h = _out_proj(h, [attn], attn_w_o[j].astype(BF16))
        last = layer == depth - 1
        h = _mlp(h, norm_mlp[layer], mlp_w1, mlp_w2, layer, final_norm if last else None)
    if depth == 0:
        h = _rms_norm(h, final_norm)
    return h.reshape(batch, seq, d)
```

```python
import functools
import math

import jax
import jax.numpy as jnp
from jax import lax
from jax.experimental import pallas as pl
from jax.experimental.pallas import tpu as pltpu

F32 = jnp.float32
BF16 = jnp.bfloat16

NORM_EPS = 1e-6
LRU_C = 8.0
LRU_HEADS = 4
CONV_WIDTH = 4
HGRN_HEAD = 128
ATTN_HEADS = 16
HEAD_DIM = 128
ROPE_DIM = 32
ROPE_THETA = 500000.0
DILATIONS = (1, 4, 16)
ATTN_BLOCK = 128
ATTN_RES = 16
LANES = 128
MXU_COLS = 256
NORM_ROW_CHUNK = 256
LRU_ROW_CHUNK = 128
QKV_CHUNK_HEADS = 2
CONV_HALO = 8

VMEM_LIMIT = 56 * 1024 * 1024


def _params(*sem):
    return pltpu.CompilerParams(dimension_semantics=sem, vmem_limit_bytes=VMEM_LIMIT)


def _rms_norm(x, gain):
    return x * lax.rsqrt(jnp.mean(x * x, axis=-1, keepdims=True) + NORM_EPS) * gain


def _dot(a, b):
    return jnp.dot(a, b, preferred_element_type=F32)


def _dot_nt(a, b):
    return lax.dot_general(a, b, (((1,), (1,)), ((), ())), preferred_element_type=F32)


def _dot_tn(a, b):
    return lax.dot_general(a, b, (((0,), (0,)), ((), ())), preferred_element_type=F32)


def _qkv_kernel(x_ref, g_ref, w_ref, pos_ref, freq_ref, sign_ref, o_ref, xn_ref, cos_ref, sin_ref,
                *, q_tiles, qk_tiles, heads_per_tile):
    j = pl.program_id(1)
    half = ROPE_DIM // 2

    def rotated_tile(rows, xn, cos, sin):
        lane = lax.broadcasted_iota(jnp.int32, cos.shape, 1)
        for c in range(heads_per_tile // QKV_CHUNK_HEADS):
            c0 = c * QKV_CHUNK_HEADS * HEAD_DIM
            acc = _dot(xn, w_ref[:, c0:c0 + QKV_CHUNK_HEADS * HEAD_DIM])
            for hh in range(QKV_CHUNK_HEADS):
                t = acc[:, hh * HEAD_DIM:(hh + 1) * HEAD_DIM]
                partner = jnp.where(lane < half, pltpu.roll(t, HEAD_DIM - half, axis=1),
                                    pltpu.roll(t, half, axis=1))
                o_ref[rows, c0 + hh * HEAD_DIM:c0 + (hh + 1) * HEAD_DIM] = t * cos + partner * sin

    @pl.when(j == 0)
    def _():
        for r in range(0, x_ref.shape[0], NORM_ROW_CHUNK):
            rows = pl.ds(r, NORM_ROW_CHUNK)
            xn = _rms_norm(x_ref[rows, :], g_ref[...]).astype(BF16)
            xn_ref[rows, :] = xn
            ang = pos_ref[rows, :] * freq_ref[...]
            cos = jnp.cos(ang)
            sin = jnp.sin(ang) * sign_ref[...]
            scale = F32(HEAD_DIM ** -0.5 * math.log2(math.e))
            cos_ref[0, rows, :] = cos * scale
            sin_ref[0, rows, :] = sin * scale
            cos_ref[1, rows, :] = cos
            sin_ref[1, rows, :] = sin
            rotated_tile(rows, xn, cos * scale, sin * scale)

    @pl.when((j > 0) & (j < qk_tiles))
    def _():
        group = (j >= q_tiles).astype(jnp.int32)
        rotated_tile(slice(None), xn_ref[...], cos_ref[group], sin_ref[group])

    @pl.when(j >= qk_tiles)
    def _():
        o_ref[...] = _dot(xn_ref[...], w_ref[...])


def _qkv_proj(x, gain, w, pos, *, tm=1024, tn=1024):
    m, d = x.shape
    n = w.shape[1]
    half = ROPE_DIM // 2
    inv_freq = 1.0 / (ROPE_THETA ** (jnp.arange(half, dtype=F32) * (2.0 / ROPE_DIM)))
    zeros = jnp.zeros((HEAD_DIM - ROPE_DIM,), F32)
    freq = jnp.concatenate([inv_freq, inv_freq, zeros]).reshape(1, HEAD_DIM)
    sign = jnp.concatenate([-jnp.ones((half,), F32), jnp.ones((half,), F32), zeros]).reshape(1, HEAD_DIM)
    d_attn = n // 3
    kern = functools.partial(_qkv_kernel, q_tiles=d_attn // tn, qk_tiles=2 * d_attn // tn,
                             heads_per_tile=tn // HEAD_DIM)
    return pl.pallas_call(
        kern,
        grid=(m // tm, n // tn),
        in_specs=[pl.BlockSpec((tm, d), lambda i, j: (i, 0)),
                  pl.BlockSpec((1, d), lambda i, j: (0, 0)),
                  pl.BlockSpec((d, tn), lambda i, j: (0, j)),
                  pl.BlockSpec((tm, 1), lambda i, j: (i, 0)),
                  pl.BlockSpec((1, HEAD_DIM), lambda i, j: (0, 0)),
                  pl.BlockSpec((1, HEAD_DIM), lambda i, j: (0, 0))],
        out_specs=pl.BlockSpec((tm, tn), lambda i, j: (i, j)),
        out_shape=jax.ShapeDtypeStruct((m, n), F32),
        scratch_shapes=[pltpu.VMEM((tm, d), BF16), pltpu.VMEM((2, tm, HEAD_DIM), F32),
                        pltpu.VMEM((2, tm, HEAD_DIM), F32)],
        compiler_params=_params("parallel", "arbitrary"),
        name="qkv_rope",
    )(x, gain.reshape(1, d), w, pos.astype(F32).reshape(m, 1), freq, sign)


def _out_proj_kernel(*refs, n_parts):
    h_ref = refs[0]
    a_refs = refs[1:1 + n_parts]
    w_refs = refs[1 + n_parts:1 + 2 * n_parts]
    o_ref = refs[1 + 2 * n_parts]
    acc = h_ref[...]
    for a_ref, w_ref in zip(a_refs, w_refs):
        acc = acc + _dot(a_ref[...].astype(BF16), w_ref[...])
    o_ref[...] = acc


def _out_proj(h, parts, w, *, tm=512):
    m, n = h.shape
    k_part = parts[0].shape[1]
    assert all(a.shape[1] == k_part for a in parts) and k_part * len(parts) == w.shape[0]
    row = pl.BlockSpec((tm, n), lambda i: (i, 0))
    in_specs = [row]
    in_specs += [pl.BlockSpec((tm, k_part), lambda i: (i, 0)) for _ in parts]
    in_specs += [pl.BlockSpec((k_part, n), lambda i, p=p: (p, 0)) for p in range(len(parts))]
    return pl.pallas_call(
        functools.partial(_out_proj_kernel, n_parts=len(parts)),
        grid=(m // tm,),
        in_specs=in_specs,
        out_specs=row,
        out_shape=jax.ShapeDtypeStruct((m, n), F32),
        compiler_params=_params("parallel"),
        name="out_proj",
    )(h, *parts, *([w] * len(parts)))


def _mlp_kernel(h_ref, g_ref, w1_ref, w2_ref, fg_ref, o_ref, xn_ref, *, final_norm):
    k = pl.program_id(1)
    last = pl.num_programs(1) - 1
    row_chunks = [pl.ds(r, NORM_ROW_CHUNK) for r in range(0, h_ref.shape[0], NORM_ROW_CHUNK)]

    def contribution(xn):
        a = jnp.maximum(_dot(xn, w1_ref[...]), 0.0)
        return _dot((a * a).astype(BF16), w2_ref[...])

    @pl.when(k == 0)
    def _():
        for rows in row_chunks:
            x = h_ref[rows, :]
            xn = _rms_norm(x, g_ref[...]).astype(BF16)
            xn_ref[rows, :] = xn
            o_ref[rows, :] = x + contribution(xn)

    @pl.when((k > 0) & (k < last) if final_norm else k > 0)
    def _():
        o_ref[...] += contribution(xn_ref[...])

    if final_norm:
        @pl.when(k == last)
        def _():
            for rows in row_chunks:
                acc = o_ref[rows, :] + contribution(xn_ref[rows, :])
                o_ref[rows, :] = _rms_norm(acc, fg_ref[...])


def _mlp(h, gain, w1, w2, layer, final_gain=None, *, tm=1024, tf=512):
    m, d = h.shape
    f = w1.shape[2]
    fg = jnp.ones((d,), F32) if final_gain is None else final_gain
    return pl.pallas_call(
        functools.partial(_mlp_kernel, final_norm=final_gain is not None),
        grid=(m // tm, f // tf),
        in_specs=[pl.BlockSpec((tm, d), lambda i, k: (i, 0)),
                  pl.BlockSpec((1, d), lambda i, k: (0, 0)),
                  pl.BlockSpec((None, d, tf), lambda i, k: (layer, 0, k)),
                  pl.BlockSpec((None, tf, d), lambda i, k: (layer, k, 0)),
                  pl.BlockSpec((1, d), lambda i, k: (0, 0))],
        out_specs=pl.BlockSpec((tm, d), lambda i, k: (i, 0)),
        out_shape=jax.ShapeDtypeStruct((m, d), F32),
        scratch_shapes=[pltpu.VMEM((tm, d), BF16)],
        compiler_params=_params("parallel", "arbitrary"),
        name="mlp",
    )(h, gain.reshape(1, d), w1, w2, fg.reshape(1, d))


SUBLANES = 8


def _linear_scan(a, u, carry):
    rows, width = a.shape
    groups = rows // SUBLANES
    a = a.reshape(groups, SUBLANES, width)
    u = u.reshape(groups, SUBLANES, width)
    sub = lax.broadcasted_iota(jnp.int32, a.shape, 1)
    d = 1
    while d < SUBLANES:
        keep = sub >= d
        u = u + a * jnp.where(keep, pltpu.roll(u, d, axis=1), 0.0)
        a = a * jnp.where(keep, pltpu.roll(a, d, axis=1), 1.0)
        d *= 2
    out = []
    for g in range(groups):
        h = u[g] + a[g] * carry
        carry = h[SUBLANES - 1:SUBLANES]
        out.append(h)
    return jnp.concatenate(out, axis=0), carry


def _lru_rows(hh, r0, n_rows, xbuf_ref, ybuf_ref, cw_ref, cb_ref, wa_ref, ba_ref, wi_ref, bi_ref, lam_ref,
              carry):
    blk = ybuf_ref.shape[1] // LRU_HEADS
    cols = slice(hh * blk, (hh + 1) * blk)
    tap0 = CONV_HALO - CONV_WIDTH + 1 + r0
    xc = cb_ref[:, cols] + cw_ref[0:1, cols] * xbuf_ref[pl.ds(tap0, n_rows), cols]
    for j in range(1, CONV_WIDTH):
        xc = xc + cw_ref[j:j + 1, cols] * xbuf_ref[pl.ds(tap0 + j, n_rows), cols]

    xh = xc.astype(BF16)
    r = jax.nn.sigmoid(_dot(xh, wa_ref[hh]) + ba_ref[:, cols])
    gi = jax.nn.sigmoid(_dot(xh, wi_ref[hh]) + bi_ref[:, cols])

    neg_lam = -lam_ref[:, cols]
    softplus = jnp.maximum(neg_lam, 0.0) + jnp.log1p(jnp.exp(-jnp.abs(neg_lam)))
    log_a = (-LRU_C) * r * softplus
    a = jnp.exp(log_a)
    u = jnp.sqrt(-jnp.tanh(log_a) * (a * a + 1.0)) * (gi * xc)

    h, carry = _linear_scan(a, u, carry)

    y = ybuf_ref[pl.ds(r0, n_rows), cols]
    gelu = 0.5 * y * (1.0 + jnp.tanh(math.sqrt(2.0 / math.pi) * (y + 0.044715 * (y * y * y))))
    return (h * gelu).astype(BF16), carry


def _in_proj_lru_kernel(x_ref, g_ref, w_ref, cw_ref, cb_ref, wa_ref, ba_ref, wi_ref, bi_ref, lam_ref,
                        proj_ref, lru_ref, xn_ref, xbuf_ref, ybuf_ref, carry_ref, *, tiles_per_seq):
    i = pl.program_id(0)
    j = pl.program_id(1)
    tm = x_ref.shape[0]

    @pl.when(j == 0)
    def _():
        @pl.when(i % tiles_per_seq == 0)
        def _():
            xbuf_ref[0:CONV_HALO, :] = jnp.zeros((CONV_HALO, xbuf_ref.shape[1]), F32)
            carry_ref[...] = jnp.zeros(carry_ref.shape, F32)

        @pl.when(i % tiles_per_seq != 0)
        def _():
            xbuf_ref[0:CONV_HALO, :] = xbuf_ref[tm:tm + CONV_HALO, :]

        for r in range(0, tm, NORM_ROW_CHUNK):
            rows = pl.ds(r, NORM_ROW_CHUNK)
            xn = _rms_norm(x_ref[rows, :], g_ref[...]).astype(BF16)
            xn_ref[rows, :] = xn
            xbuf_ref[pl.ds(CONV_HALO + r, NORM_ROW_CHUNK), :] = _dot(xn, w_ref[...])

    @pl.when(j == 1)
    def _():
        ybuf_ref[...] = _dot(xn_ref[...], w_ref[...])

    for hh in range(LRU_HEADS):
        @pl.when(j == 2 + hh)
        def _(hh=hh):
            blk = ybuf_ref.shape[1] // LRU_HEADS
            cols = slice(hh * blk, (hh + 1) * blk)
            carry = carry_ref[:, cols]
            n_pieces = tm // LRU_ROW_CHUNK
            n_col_chunks = w_ref.shape[1] // MXU_COLS
            row_parts = n_pieces // n_col_chunks
            part_rows = tm // row_parts
            for piece in range(n_pieces):
                c_cols = slice((piece // row_parts) * MXU_COLS, (piece // row_parts + 1) * MXU_COLS)
                p_rows = pl.ds((piece % row_parts) * part_rows, part_rows)
                proj_ref[p_rows, c_cols] = _dot(xn_ref[p_rows, :], w_ref[:, c_cols])
                r0 = piece * LRU_ROW_CHUNK
                out, carry = _lru_rows(hh, r0, LRU_ROW_CHUNK, xbuf_ref, ybuf_ref, cw_ref, cb_ref,
                                       wa_ref, ba_ref, wi_ref, bi_ref, lam_ref, carry)
                lru_ref[pl.ds(r0, LRU_ROW_CHUNK), :] = out
            carry_ref[:, cols] = carry


def _in_proj_lru(x, gain, w, conv_w, conv_b, w_a, b_a, w_i, b_i, lam, *, seq, tm=1024):
    m, d = x.shape
    width = conv_w.shape[1]
    tn = width
    n_steps = w.shape[1] // tn
    assert n_steps == 2 + LRU_HEADS and seq % tm == 0
    blk = width // LRU_HEADS
    const2 = lambda i, j: (0, 0)
    vec = pl.BlockSpec((1, width), const2)
    gate_w = pl.BlockSpec(w_a.shape, lambda i, j: (0, 0, 0))
    late = lambda i, j: (i, jnp.maximum(j - 2, 0))
    return pl.pallas_call(
        functools.partial(_in_proj_lru_kernel, tiles_per_seq=seq // tm),
        grid=(m // tm, n_steps),
        in_specs=[pl.BlockSpec((tm, d), lambda i, j: (i, 0)),
                  pl.BlockSpec((1, d), const2),
                  pl.BlockSpec((d, tn), lambda i, j: (0, j)),
                  pl.BlockSpec((CONV_WIDTH, width), const2), vec,
                  gate_w, vec, gate_w, vec, vec],
        out_specs=[pl.BlockSpec((tm, tn), late), pl.BlockSpec((tm, blk), late)],
        out_shape=[jax.ShapeDtypeStruct((m, w.shape[1] - 2 * width), F32),
                   jax.ShapeDtypeStruct((m, width), BF16)],
        scratch_shapes=[pltpu.VMEM((tm, d), BF16), pltpu.VMEM((tm + CONV_HALO, width), F32),
                        pltpu.VMEM((tm, width), F32), pltpu.VMEM((1, width), F32)],
        compiler_params=_params("arbitrary", "arbitrary"),
        name="in_proj_rg_lru",
    )(x, gain.reshape(1, d), w, conv_w, conv_b.reshape(1, width), w_a, b_a.reshape(1, width),
      w_i, b_i.reshape(1, width), lam.reshape(1, width))


def _cumsum_rows(x):
    rows = x.shape[0]
    row = lax.broadcasted_iota(jnp.int32, x.shape, 0)
    d = 1
    while d < rows:
        x = x + jnp.where(row >= d, pltpu.roll(x, d, axis=0), 0.0)
        d *= 2
    return x


def _shift_rows(x, s):
    rows, width = x.shape
    if abs(s) < SUBLANES:
        x3 = x.reshape(rows // SUBLANES, SUBLANES, width)
        return pltpu.roll(x3, s % SUBLANES, axis=1).reshape(rows, width)
    return pltpu.roll(x, s % rows, axis=0)


def _hgrn_kernel(q_ref, f_ref, v_ref, g_ref, lbl_ref, gn_ref, o_ref, state_ref, level_ref,
                 *, layer, chunk, n_chunks):
    ti = lax.broadcasted_iota(jnp.int32, (chunk, chunk), 0)
    si = lax.broadcasted_iota(jnp.int32, (chunk, chunk), 1)
    tx = ti ^ si

    @pl.when(pl.program_id(2) == 0)
    def _():
        state_ref[...] = jnp.zeros(state_ref.shape, F32)
        for li in range(level_ref.shape[0]):
            s = 1 << li
            level_ref[li] = jnp.where((tx >= s) & (tx < 2 * s) & (ti > si), 1.0, 0.0)

    logits = lbl_ref[...]
    e = jnp.exp(logits - jnp.max(logits, axis=0, keepdims=True))
    lb = jnp.sum(e[0:layer + 1, :], axis=0, keepdims=True) / jnp.sum(e, axis=0, keepdims=True)

    row = lax.broadcasted_iota(jnp.int32, (chunk, HGRN_HEAD), 0)

    for c in range(n_chunks):
        rows = pl.ds(c * chunk, chunk)
        qr = q_ref[rows, :]
        fz = f_ref[rows, :]
        v = v_ref[rows, :].astype(BF16)
        q = qr * jax.nn.sigmoid(qr)
        log_f = jnp.log(lb + (1.0 - lb) * jax.nn.sigmoid(fz))
        kk = (1.0 - lb) * jax.nn.sigmoid(-fz)
        b = _cumsum_rows(log_f)

        scores = jnp.where(tx == 0, jnp.sum(q * kk, axis=-1, keepdims=True), 0.0)
        b_end = b
        s = 1
        for li in range(level_ref.shape[0]):
            upper = (row & s) != 0
            decay = jnp.exp(jnp.where(upper, b - _shift_rows(b_end, s), b_end - b))
            z = (jnp.where(upper, q, kk) * decay).astype(BF16)
            scores = scores + _dot_nt(z, z) * level_ref[li]
            b_end = jnp.where(upper, b_end, _shift_rows(b_end, -s))
            s *= 2

        state = state_ref[...]
        o = _dot(scores.astype(BF16), v) + _dot_nt((q * jnp.exp(b)).astype(BF16), state.astype(BF16))
        kd = (kk * jnp.exp(b_end - b)).astype(BF16)
        state_ref[...] = state * jnp.exp(b_end[0:1, :]) + _dot_tn(v, kd)

        o = o * lax.rsqrt(jnp.mean(o * o, axis=-1, keepdims=True) + NORM_EPS) * gn_ref[...]
        gr = g_ref[rows, :]
        o_ref[rows, :] = (o * (gr * jax.nn.sigmoid(gr))).astype(o_ref.dtype)


def _hgrn(proj, lb_logits, g_norm, *, batch, seq, layer, col0, tt=2048, chunk=128):
    width = g_norm.shape[0]
    heads = width // HGRN_HEAD
    nt = seq // tt
    hb = width // HGRN_HEAD
    c0 = col0 // HGRN_HEAD

    def col(group):
        return pl.BlockSpec((tt, HGRN_HEAD), lambda b, h, t: (b * nt + t, c0 + group * hb + h))

    return pl.pallas_call(
        functools.partial(_hgrn_kernel, layer=layer, chunk=chunk, n_chunks=tt // chunk),
        grid=(batch, heads, nt),
        in_specs=[col(0), col(1), col(2), col(3),
                  pl.BlockSpec((lb_logits.shape[0], HGRN_HEAD), lambda b, h, t: (0, h)),
                  pl.BlockSpec((1, HGRN_HEAD), lambda b, h, t: (0, h))],
        out_specs=pl.BlockSpec((tt, HGRN_HEAD), lambda b, h, t: (b * nt + t, h)),
        out_shape=jax.ShapeDtypeStruct((batch * seq, width), BF16),
        scratch_shapes=[pltpu.VMEM((HGRN_HEAD, HGRN_HEAD), F32),
                        pltpu.VMEM((chunk.bit_length() - 1, chunk, chunk), F32)],
        compiler_params=_params("parallel", "parallel", "arbitrary"),
        name="hgrn2",
    )(proj, proj, proj, proj, lb_logits, g_norm.reshape(1, width))


def _attn_kernel(q_ref, k_ref, v_ref, o_ref, qc_ref, kbuf_ref, vbuf_ref, num_ref, max_ref, den_ref, bias_ref,
                 *, unroll):
    t = pl.program_id(2)
    blk = ATTN_BLOCK
    res = ATTN_RES

    @pl.when(t == 0)
    def _():
        kbuf_ref[:, 0:blk, :] = jnp.zeros((res, blk, HEAD_DIM), F32)
        vbuf_ref[:, 0:blk, :] = jnp.zeros((res, blk, HEAD_DIM), F32)
        a = lax.broadcasted_iota(jnp.int32, (blk, 2 * blk), 0)
        c = lax.broadcasted_iota(jnp.int32, (blk, 2 * blk), 1)
        for bi, dil in enumerate(DILATIONS):
            runs = res // dil
            q_run = blk // runs
            k_run = 2 * blk // runs
            dist = blk + runs * (a % q_run - c % k_run) + (a // q_run - c // k_run)
            ok = (dist >= 0) & (dist <= blk)
            bias_ref[2 * bi] = jnp.where(ok, 0.0, -jnp.inf)
            bias_ref[2 * bi + 1] = jnp.where(ok & (c % k_run >= q_run), 0.0, -jnp.inf)

    @pl.when(t > 0)
    def _():
        kbuf_ref[:, 0:blk, :] = kbuf_ref[:, blk:2 * blk, :]
        vbuf_ref[:, 0:blk, :] = vbuf_ref[:, blk:2 * blk, :]

    for r in range(res):
        rows = pl.ds(r, blk, stride=res)
        qc_ref[r] = q_ref[rows, :]
        kbuf_ref[r, blk:2 * blk, :] = k_ref[rows, :]
        vbuf_ref[r, blk:2 * blk, :] = v_ref[rows, :]

    ones = jnp.ones((2 * blk, HEAD_DIM), BF16)
    for bi, dil in enumerate(DILATIONS):
        runs = res // dil
        q_run = blk // runs
        k_run = 2 * blk // runs

        def block(idx, bi=bi, dil=dil, runs=runs, q_run=q_run, k_run=k_run):
            r_d = idx // runs
            n = idx % runs
            q0 = pl.multiple_of(n * q_run, q_run)
            k0 = pl.multiple_of(blk + (n - 1) * q_run, q_run)
            planes = [r_d + dil * j for j in range(runs)]
            qb = jnp.concatenate([qc_ref[p, pl.ds(q0, q_run), :] for p in planes], axis=0).astype(BF16)
            kb = jnp.concatenate([kbuf_ref[p, pl.ds(k0, k_run), :] for p in planes], axis=0).astype(BF16)
            vb = jnp.concatenate([vbuf_ref[p, pl.ds(k0, k_run), :] for p in planes], axis=0).astype(BF16)
            first = ((t == 0) & (n == 0)).astype(jnp.int32)
            s = _dot_nt(qb, kb) + bias_ref[2 * bi + first]
            m = jnp.max(s, axis=-1, keepdims=True)
            p = jnp.exp2(s - m).astype(BF16)
            ol = _dot(p, jnp.concatenate([vb, ones], axis=1))
            o = ol[:, :HEAD_DIM]
            l = ol[:, HEAD_DIM:]
            for j, plane in enumerate(planes):
                run = slice(j * q_run, (j + 1) * q_run)
                dst = (plane, pl.ds(q0, q_run), slice(None))
                m_j = jnp.broadcast_to(m[run], (q_run, HEAD_DIM))
                if bi == 0:
                    max_ref[dst] = m_j
                    den_ref[dst] = l[run]
                    num_ref[dst] = o[run]
                else:
                    m_old = max_ref[dst]
                    m_new = jnp.maximum(m_old, m_j)
                    w_old = jnp.exp2(m_old - m_new)
                    w_new = jnp.exp2(m_j - m_new)
                    max_ref[dst] = m_new
                    den_ref[dst] = den_ref[dst] * w_old + l[run] * w_new
                    num_ref[dst] = num_ref[dst] * w_old + o[run] * w_new

        def body(i, carry, block=block):
            for u in range(unroll):
                block(i * unroll + u)
            return carry

        lax.fori_loop(0, res // unroll, body, 0)

    for r in range(res):
        o_ref[pl.ds(r, blk, stride=res), :] = num_ref[r] / den_ref[r]


def _attention(qkv, *, batch, seq, unroll=16):
    tq = ATTN_BLOCK * ATTN_RES
    nt = seq // tq
    d_attn = ATTN_HEADS * HEAD_DIM

    def col(group):
        return pl.BlockSpec((tq, HEAD_DIM), lambda b, h, t: (b * nt + t, group * ATTN_HEADS + h))

    plane = pltpu.VMEM((ATTN_RES, ATTN_BLOCK, HEAD_DIM), F32)
    band = pltpu.VMEM((ATTN_RES, 2 * ATTN_BLOCK, HEAD_DIM), F32)
    return pl.pallas_call(
        functools.partial(_attn_kernel, unroll=unroll),
        grid=(batch, ATTN_HEADS, nt),
        in_specs=[col(0), col(1), col(2)],
        out_specs=pl.BlockSpec((tq, HEAD_DIM), lambda b, h, t: (b * nt + t, h)),
        out_shape=jax.ShapeDtypeStruct((batch * seq, d_attn), F32),
        scratch_shapes=[plane, band, band, plane, plane, plane,
                        pltpu.VMEM((2 * len(DILATIONS), ATTN_BLOCK, 2 * ATTN_BLOCK), F32)],
        compiler_params=_params("parallel", "parallel", "arbitrary"),
        name="dilated_attention",
    )(qkv, qkv, qkv)


def kernel(x, positions, norm_mix, norm_mlp, final_norm, rec_w_in, rec_conv_w, rec_conv_b, lru_w_a, lru_b_a, lru_w_i, lru_b_i, lru_lambda, hgrn_lb_logits, hgrn_g_norm, rec_w_out, attn_w_qkv, attn_w_o, mlp_w1, mlp_w2):
    batch, seq, d = x.shape
    depth = norm_mix.shape[0]
    assert seq % (ATTN_BLOCK * ATTN_RES) == 0 and DILATIONS[-1] == ATTN_RES
    h = x.reshape(batch * seq, d)
    pos = positions.reshape(batch * seq)
    mlp_w1 = mlp_w1.astype(BF16)
    mlp_w2 = mlp_w2.astype(BF16)
    for layer in range(depth):
        j = layer // 2
        if layer % 2 == 0:
            proj, lru = _in_proj_lru(h, norm_mix[layer], rec_w_in[j].astype(BF16), rec_conv_w[j],
                                     rec_conv_b[j], lru_w_a[j].astype(BF16), lru_b_a[j],
                                     lru_w_i[j].astype(BF16), lru_b_i[j], lru_lambda[j], seq=seq)
            hg = _hgrn(proj, hgrn_lb_logits, hgrn_g_norm[j], batch=batch, seq=seq, layer=layer, col0=0)
            h = _out_proj(h, [lru, hg], rec_w_out[j].astype(BF16))
        else:
            qkv = _qkv_proj(h, norm_mix[layer], attn_w_qkv[j].astype(BF16), pos)
            attn = _attention(qkv, batch=batch, seq=seq)
            h = _out_proj(h, [attn], attn_w_o[j].astype(BF16))
        last = layer == depth - 1
        h = _mlp(h, norm_mlp[layer], mlp_w1, mlp_w2, layer, final_norm if last else None)
    if depth == 0:
        h = _rms_norm(h, final_norm)
    return h.reshape(batch, seq, d)
```

```python
import functools
import math

import jax
import jax.numpy as jnp
from jax import lax
from jax.experimental import pallas as pl
from jax.experimental.pallas import tpu as pltpu

F32 = jnp.float32
BF16 = jnp.bfloat16

NORM_EPS = 1e-6
LRU_C = 8.0
LRU_HEADS = 4
CONV_WIDTH = 4
HGRN_HEAD = 128
ATTN_HEADS = 16
HEAD_DIM = 128
ROPE_DIM = 32
ROPE_THETA = 500000.0
DILATIONS = (1, 4, 16)
ATTN_BLOCK = 128
ATTN_RES = 16
LANES = 128
MXU_COLS = 256
NORM_ROW_CHUNK = 256
LRU_ROW_CHUNK = 64
QKV_ROW_CHUNK = 256
QKV_CHUNK_HEADS = 2
CONV_HALO = 8

VMEM_LIMIT = 56 * 1024 * 1024


def _params(*sem):
    return pltpu.CompilerParams(dimension_semantics=sem, vmem_limit_bytes=VMEM_LIMIT)


def _rms_norm(x, gain):
    return x * lax.rsqrt(jnp.mean(x * x, axis=-1, keepdims=True) + NORM_EPS) * gain


def _dot(a, b):
    return jnp.dot(a, b, preferred_element_type=F32)


def _dot_nt(a, b):
    return lax.dot_general(a, b, (((1,), (1,)), ((), ())), preferred_element_type=F32)


def _dot_tn(a, b):
    return lax.dot_general(a, b, (((0,), (0,)), ((), ())), preferred_element_type=F32)


def _qkv_kernel(x_ref, g_ref, w_ref, pos_ref, freq_ref, sign_ref, o_ref, xn_ref, cos_ref, sin_ref,
                *, q_tiles, qk_tiles, heads_per_tile):
    j = pl.program_id(1)
    half = ROPE_DIM // 2

    def rotated_tile(rows, xn, cos, sin):
        lane = lax.broadcasted_iota(jnp.int32, cos.shape, 1)
        for c in range(heads_per_tile // QKV_CHUNK_HEADS):
            c0 = c * QKV_CHUNK_HEADS * HEAD_DIM
            acc = _dot(xn, w_ref[:, c0:c0 + QKV_CHUNK_HEADS * HEAD_DIM])
            for hh in range(QKV_CHUNK_HEADS):
                t = acc[:, hh * HEAD_DIM:(hh + 1) * HEAD_DIM]
                partner = jnp.where(lane < half, pltpu.roll(t, HEAD_DIM - half, axis=1),
                                    pltpu.roll(t, half, axis=1))
                o_ref[rows, c0 + hh * HEAD_DIM:c0 + (hh + 1) * HEAD_DIM] = t * cos + partner * sin

    @pl.when(j == 0)
    def _():
        for r in range(0, x_ref.shape[0], NORM_ROW_CHUNK):
            rows = pl.ds(r, NORM_ROW_CHUNK)
            xn = _rms_norm(x_ref[rows, :], g_ref[...]).astype(BF16)
            xn_ref[rows, :] = xn
            ang = pos_ref[rows, :] * freq_ref[...]
            cos = jnp.cos(ang)
            sin = jnp.sin(ang) * sign_ref[...]
            scale = F32(HEAD_DIM ** -0.5 * math.log2(math.e))
            cos_ref[0, rows, :] = cos * scale
            sin_ref[0, rows, :] = sin * scale
            cos_ref[1, rows, :] = cos
            sin_ref[1, rows, :] = sin
            rotated_tile(rows, xn, cos * scale, sin * scale)

    @pl.when((j > 0) & (j < qk_tiles))
    def _():
        group = (j >= q_tiles).astype(jnp.int32)
        for r in range(0, x_ref.shape[0], QKV_ROW_CHUNK):
            rows = pl.ds(r, QKV_ROW_CHUNK)
            rotated_tile(rows, xn_ref[rows, :], cos_ref[group, rows, :], sin_ref[group, rows, :])

    @pl.when(j >= qk_tiles)
    def _():
        o_ref[...] = _dot(xn_ref[...], w_ref[...])


def _qkv_proj(x, gain, w, pos, *, tm=1024, tn=1024):
    m, d = x.shape
    n = w.shape[1]
    half = ROPE_DIM // 2
    inv_freq = 1.0 / (ROPE_THETA ** (jnp.arange(half, dtype=F32) * (2.0 / ROPE_DIM)))
    zeros = jnp.zeros((HEAD_DIM - ROPE_DIM,), F32)
    freq = jnp.concatenate([inv_freq, inv_freq, zeros]).reshape(1, HEAD_DIM)
    sign = jnp.concatenate([-jnp.ones((half,), F32), jnp.ones((half,), F32), zeros]).reshape(1, HEAD_DIM)
    d_attn = n // 3
    kern = functools.partial(_qkv_kernel, q_tiles=d_attn // tn, qk_tiles=2 * d_attn // tn,
                             heads_per_tile=tn // HEAD_DIM)
    return pl.pallas_call(
        kern,
        grid=(m // tm, n // tn),
        in_specs=[pl.BlockSpec((tm, d), lambda i, j: (i, 0)),
                  pl.BlockSpec((1, d), lambda i, j: (0, 0)),
                  pl.BlockSpec((d, tn), lambda i, j: (0, j)),
                  pl.BlockSpec((tm, 1), lambda i, j: (i, 0)),
                  pl.BlockSpec((1, HEAD_DIM), lambda i, j: (0, 0)),
                  pl.BlockSpec((1, HEAD_DIM), lambda i, j: (0, 0))],
        out_specs=pl.BlockSpec((tm, tn), lambda i, j: (i, j)),
        out_shape=jax.ShapeDtypeStruct((m, n), F32),
        scratch_shapes=[pltpu.VMEM((tm, d), BF16), pltpu.VMEM((2, tm, HEAD_DIM), F32),
                        pltpu.VMEM((2, tm, HEAD_DIM), F32)],
        compiler_params=_params("parallel", "arbitrary"),
        name="qkv_rope",
    )(x, gain.reshape(1, d), w, pos.astype(F32).reshape(m, 1), freq, sign)


def _out_proj_kernel(*refs, n_parts):
    h_ref = refs[0]
    a_refs = refs[1:1 + n_parts]
    w_refs = refs[1 + n_parts:1 + 2 * n_parts]
    o_ref = refs[1 + 2 * n_parts]
    acc = h_ref[...]
    for a_ref, w_ref in zip(a_refs, w_refs):
        acc = acc + _dot(a_ref[...].astype(BF16), w_ref[...])
    o_ref[...] = acc


def _out_proj(h, parts, w, *, tm=512):
    m, n = h.shape
    k_part = parts[0].shape[1]
    assert all(a.shape[1] == k_part for a in parts) and k_part * len(parts) == w.shape[0]
    row = pl.BlockSpec((tm, n), lambda i: (i, 0))
    in_specs = [row]
    in_specs += [pl.BlockSpec((tm, k_part), lambda i: (i, 0)) for _ in parts]
    in_specs += [pl.BlockSpec((k_part, n), lambda i, p=p: (p, 0)) for p in range(len(parts))]
    return pl.pallas_call(
        functools.partial(_out_proj_kernel, n_parts=len(parts)),
        grid=(m // tm,),
        in_specs=in_specs,
        out_specs=row,
        out_shape=jax.ShapeDtypeStruct((m, n), F32),
        compiler_params=_params("parallel"),
        name="out_proj",
    )(h, *parts, *([w] * len(parts)))


def _mlp_kernel(h_ref, g_ref, w1_ref, w2_ref, fg_ref, o_ref, xn_ref, *, final_norm):
    k = pl.program_id(1)
    last = pl.num_programs(1) - 1
    row_chunks = [pl.ds(r, NORM_ROW_CHUNK) for r in range(0, h_ref.shape[0], NORM_ROW_CHUNK)]

    def contribution(xn):
        a = jnp.maximum(_dot(xn, w1_ref[...]), 0.0)
        return _dot((a * a).astype(BF16), w2_ref[...])

    @pl.when(k == 0)
    def _():
        for rows in row_chunks:
            x = h_ref[rows, :]
            xn = _rms_norm(x, g_ref[...]).astype(BF16)
            xn_ref[rows, :] = xn
            o_ref[rows, :] = x + contribution(xn)

    @pl.when((k > 0) & (k < last) if final_norm else k > 0)
    def _():
        o_ref[...] += contribution(xn_ref[...])

    if final_norm:
        @pl.when(k == last)
        def _():
            for rows in row_chunks:
                acc = o_ref[rows, :] + contribution(xn_ref[rows, :])
                o_ref[rows, :] = _rms_norm(acc, fg_ref[...])


def _mlp(h, gain, w1, w2, layer, final_gain=None, *, tm=1024, tf=512):
    m, d = h.shape
    f = w1.shape[2]
    fg = jnp.ones((d,), F32) if final_gain is None else final_gain
    return pl.pallas_call(
        functools.partial(_mlp_kernel, final_norm=final_gain is not None),
        grid=(m // tm, f // tf),
        in_specs=[pl.BlockSpec((tm, d), lambda i, k: (i, 0)),
                  pl.BlockSpec((1, d), lambda i, k: (0, 0)),
                  pl.BlockSpec((None, d, tf), lambda i, k: (layer, 0, k)),
                  pl.BlockSpec((None, tf, d), lambda i, k: (layer, k, 0)),
                  pl.BlockSpec((1, d), lambda i, k: (0, 0))],
        out_specs=pl.BlockSpec((tm, d), lambda i, k: (i, 0)),
        out_shape=jax.ShapeDtypeStruct((m, d), F32),
        scratch_shapes=[pltpu.VMEM((tm, d), BF16)],
        compiler_params=_params("parallel", "arbitrary"),
        name="mlp",
    )(h, gain.reshape(1, d), w1, w2, fg.reshape(1, d))


SUBLANES = 8


def _linear_scan(a, u, carry):
    rows, width = a.shape
    groups = rows // SUBLANES
    a = a.reshape(groups, SUBLANES, width)
    u = u.reshape(groups, SUBLANES, width)
    sub = lax.broadcasted_iota(jnp.int32, a.shape, 1)
    d = 1
    while d < SUBLANES:
        keep = sub >= d
        u = u + a * jnp.where(keep, pltpu.roll(u, d, axis=1), 0.0)
        a = a * jnp.where(keep, pltpu.roll(a, d, axis=1), 1.0)
        d *= 2
    out = []
    for g in range(groups):
        h = u[g] + a[g] * carry
        carry = h[SUBLANES - 1:SUBLANES]
        out.append(h)
    return jnp.concatenate(out, axis=0), carry


def _lru_rows(hh, r0, n_rows, xbuf_ref, ybuf_ref, cw_ref, cb_ref, wa_ref, ba_ref, wi_ref, bi_ref, lam_ref,
              carry):
    blk = ybuf_ref.shape[1] // LRU_HEADS
    cols = slice(hh * blk, (hh + 1) * blk)
    tap0 = CONV_HALO - CONV_WIDTH + 1 + r0
    xc = cb_ref[:, cols] + cw_ref[0:1, cols] * xbuf_ref[pl.ds(tap0, n_rows), cols]
    for j in range(1, CONV_WIDTH):
        xc = xc + cw_ref[j:j + 1, cols] * xbuf_ref[pl.ds(tap0 + j, n_rows), cols]

    xh = xc.astype(BF16)
    r = jax.nn.sigmoid(_dot(xh, wa_ref[hh]) + ba_ref[:, cols])
    gi = jax.nn.sigmoid(_dot(xh, wi_ref[hh]) + bi_ref[:, cols])

    neg_lam = -lam_ref[:, cols]
    softplus = jnp.maximum(neg_lam, 0.0) + jnp.log1p(jnp.exp(-jnp.abs(neg_lam)))
    log_a = (-LRU_C) * r * softplus
    a = jnp.exp(log_a)
    u = jnp.sqrt(-jnp.tanh(log_a) * (a * a + 1.0)) * (gi * xc)

    h, carry = _linear_scan(a, u, carry)

    y = ybuf_ref[pl.ds(r0, n_rows), cols]
    gelu = 0.5 * y * (1.0 + jnp.tanh(math.sqrt(2.0 / math.pi) * (y + 0.044715 * (y * y * y))))
    return (h * gelu).astype(BF16), carry


def _in_proj_lru_kernel(x_ref, g_ref, w_ref, cw_ref, cb_ref, wa_ref, ba_ref, wi_ref, bi_ref, lam_ref,
                        proj_ref, lru_ref, xn_ref, xbuf_ref, ybuf_ref, carry_ref, *, tiles_per_seq):
    i = pl.program_id(0)
    j = pl.program_id(1)
    tm = x_ref.shape[0]

    @pl.when(j == 0)
    def _():
        @pl.when(i % tiles_per_seq == 0)
        def _():
            xbuf_ref[0:CONV_HALO, :] = jnp.zeros((CONV_HALO, xbuf_ref.shape[1]), F32)
            carry_ref[...] = jnp.zeros(carry_ref.shape, F32)

        @pl.when(i % tiles_per_seq != 0)
        def _():
            xbuf_ref[0:CONV_HALO, :] = xbuf_ref[tm:tm + CONV_HALO, :]

        for r in range(0, tm, NORM_ROW_CHUNK):
            rows = pl.ds(r, NORM_ROW_CHUNK)
            xn = _rms_norm(x_ref[rows, :], g_ref[...]).astype(BF16)
            xn_ref[rows, :] = xn
            xbuf_ref[pl.ds(CONV_HALO + r, NORM_ROW_CHUNK), :] = _dot(xn, w_ref[...])

    @pl.when(j == 1)
    def _():
        ybuf_ref[...] = _dot(xn_ref[...], w_ref[...])

    for hh in range(LRU_HEADS):
        @pl.when(j == 2 + hh)
        def _(hh=hh):
            blk = ybuf_ref.shape[1] // LRU_HEADS
            cols = slice(hh * blk, (hh + 1) * blk)
            carry = carry_ref[:, cols]
            n_pieces = tm // LRU_ROW_CHUNK
            n_col_chunks = w_ref.shape[1] // MXU_COLS
            row_parts = n_pieces // n_col_chunks
            part_rows = tm // row_parts
            for piece in range(n_pieces):
                c_cols = slice((piece // row_parts) * MXU_COLS, (piece // row_parts + 1) * MXU_COLS)
                p_rows = pl.ds((piece % row_parts) * part_rows, part_rows)
                proj_ref[p_rows, c_cols] = _dot(xn_ref[p_rows, :], w_ref[:, c_cols])
                r0 = piece * LRU_ROW_CHUNK
                out, carry = _lru_rows(hh, r0, LRU_ROW_CHUNK, xbuf_ref, ybuf_ref, cw_ref, cb_ref,
                                       wa_ref, ba_ref, wi_ref, bi_ref, lam_ref, carry)
                lru_ref[pl.ds(r0, LRU_ROW_CHUNK), :] = out
            carry_ref[:, cols] = carry


def _in_proj_lru(x, gain, w, conv_w, conv_b, w_a, b_a, w_i, b_i, lam, *, seq, tm=1024):
    m, d = x.shape
    width = conv_w.shape[1]
    tn = width
    n_steps = w.shape[1] // tn
    assert n_steps == 2 + LRU_HEADS and seq % tm == 0
    blk = width // LRU_HEADS
    const2 = lambda i, j: (0, 0)
    vec = pl.BlockSpec((1, width), const2)
    gate_w = pl.BlockSpec(w_a.shape, lambda i, j: (0, 0, 0))
    late = lambda i, j: (i, jnp.maximum(j - 2, 0))
    return pl.pallas_call(
        functools.partial(_in_proj_lru_kernel, tiles_per_seq=seq // tm),
        grid=(m // tm, n_steps),
        in_specs=[pl.BlockSpec((tm, d), lambda i, j: (i, 0)),
                  pl.BlockSpec((1, d), const2),
                  pl.BlockSpec((d, tn), lambda i, j: (0, j)),
                  pl.BlockSpec((CONV_WIDTH, width), const2), vec,
                  gate_w, vec, gate_w, vec, vec],
        out_specs=[pl.BlockSpec((tm, tn), late), pl.BlockSpec((tm, blk), late)],
        out_shape=[jax.ShapeDtypeStruct((m, w.shape[1] - 2 * width), F32),
                   jax.ShapeDtypeStruct((m, width), BF16)],
        scratch_shapes=[pltpu.VMEM((tm, d), BF16), pltpu.VMEM((tm + CONV_HALO, width), F32),
                        pltpu.VMEM((tm, width), F32), pltpu.VMEM((1, width), F32)],
        compiler_params=_params("arbitrary", "arbitrary"),
        name="in_proj_rg_lru",
    )(x, gain.reshape(1, d), w, conv_w, conv_b.reshape(1, width), w_a, b_a.reshape(1, width),
      w_i, b_i.reshape(1, width), lam.reshape(1, width))


def _cumsum_rows(x):
    rows = x.shape[0]
    row = lax.broadcasted_iota(jnp.int32, x.shape, 0)
    d = 1
    while d < rows:
        x = x + jnp.where(row >= d, pltpu.roll(x, d, axis=0), 0.0)
        d *= 2
    return x


def _shift_rows(x, s):
    rows, width = x.shape
    if abs(s) < SUBLANES:
        x3 = x.reshape(rows // SUBLANES, SUBLANES, width)
        return pltpu.roll(x3, s % SUBLANES, axis=1).reshape(rows, width)
    return pltpu.roll(x, s % rows, axis=0)


def _hgrn_kernel(q_ref, f_ref, v_ref, g_ref, lbl_ref, gn_ref, o_ref, state_ref, level_ref,
                 *, layer, chunk, n_chunks):
    ti = lax.broadcasted_iota(jnp.int32, (chunk, chunk), 0)
    si = lax.broadcasted_iota(jnp.int32, (chunk, chunk), 1)
    tx = ti ^ si

    @pl.when(pl.program_id(2) == 0)
    def _():
        state_ref[...] = jnp.zeros(state_ref.shape, F32)
        for li in range(level_ref.shape[0]):
            s = 1 << li
            level_ref[li] = jnp.where((tx >= s) & (tx < 2 * s) & (ti > si), 1.0, 0.0)

    logits = lbl_ref[...]
    e = jnp.exp(logits - jnp.max(logits, axis=0, keepdims=True))
    lb = jnp.sum(e[0:layer + 1, :], axis=0, keepdims=True) / jnp.sum(e, axis=0, keepdims=True)

    row = lax.broadcasted_iota(jnp.int32, (chunk, HGRN_HEAD), 0)

    for c in range(n_chunks):
        rows = pl.ds(c * chunk, chunk)
        qr = q_ref[rows, :]
        fz = f_ref[rows, :]
        v = v_ref[rows, :].astype(BF16)
        q = qr * jax.nn.sigmoid(qr)
        log_f = jnp.log(lb + (1.0 - lb) * jax.nn.sigmoid(fz))
        kk = (1.0 - lb) * jax.nn.sigmoid(-fz)
        b = _cumsum_rows(log_f)

        scores = jnp.where(tx == 0, jnp.sum(q * kk, axis=-1, keepdims=True), 0.0)
        b_end = b
        s = 1
        for li in range(level_ref.shape[0]):
            upper = (row & s) != 0
            decay = jnp.exp(jnp.where(upper, b - _shift_rows(b_end, s), b_end - b))
            z = (jnp.where(upper, q, kk) * decay).astype(BF16)
            scores = scores + _dot_nt(z, z) * level_ref[li]
            b_end = jnp.where(upper, b_end, _shift_rows(b_end, -s))
            s *= 2

        state = state_ref[...]
        o = _dot(scores.astype(BF16), v) + _dot_nt((q * jnp.exp(b)).astype(BF16), state.astype(BF16))
        kd = (kk * jnp.exp(b_end - b)).astype(BF16)
        state_ref[...] = state * jnp.exp(b_end[0:1, :]) + _dot_tn(v, kd)

        o = o * lax.rsqrt(jnp.mean(o * o, axis=-1, keepdims=True) + NORM_EPS) * gn_ref[...]
        gr = g_ref[rows, :]
        o_ref[rows, :] = (o * (gr * jax.nn.sigmoid(gr))).astype(o_ref.dtype)


def _hgrn(proj, lb_logits, g_norm, *, batch, seq, layer, col0, tt=2048, chunk=128):
    width = g_norm.shape[0]
    heads = width // HGRN_HEAD
    nt = seq // tt
    hb = width // HGRN_HEAD
    c0 = col0 // HGRN_HEAD

    def col(group):
        return pl.BlockSpec((tt, HGRN_HEAD), lambda b, h, t: (b * nt + t, c0 + group * hb + h))

    return pl.pallas_call(
        functools.partial(_hgrn_kernel, layer=layer, chunk=chunk, n_chunks=tt // chunk),
        grid=(batch, heads, nt),
        in_specs=[col(0), col(1), col(2), col(3),
                  pl.BlockSpec((lb_logits.shape[0], HGRN_HEAD), lambda b, h, t: (0, h)),
                  pl.BlockSpec((1, HGRN_HEAD), lambda b, h, t: (0, h))],
        out_specs=pl.BlockSpec((tt, HGRN_HEAD), lambda b, h, t: (b * nt + t, h)),
        out_shape=jax.ShapeDtypeStruct((batch * seq, width), BF16),
        scratch_shapes=[pltpu.VMEM((HGRN_HEAD, HGRN_HEAD), F32),
                        pltpu.VMEM((chunk.bit_length() - 1, chunk, chunk), F32)],
        compiler_params=_params("parallel", "parallel", "arbitrary"),
        name="hgrn2",
    )(proj, proj, proj, proj, lb_logits, g_norm.reshape(1, width))


def _attn_kernel(q_ref, k_ref, v_ref, o_ref, qc_ref, kbuf_ref, vbuf_ref, num_ref, max_ref, den_ref, bias_ref,
                 *, unroll):
    t = pl.program_id(2)
    blk = ATTN_BLOCK
    res = ATTN_RES

    @pl.when(t == 0)
    def _():
        kbuf_ref[:, 0:blk, :] = jnp.zeros((res, blk, HEAD_DIM), F32)
        vbuf_ref[:, 0:blk, :] = jnp.zeros((res, blk, HEAD_DIM), F32)
        a = lax.broadcasted_iota(jnp.int32, (blk, 2 * blk), 0)
        c = lax.broadcasted_iota(jnp.int32, (blk, 2 * blk), 1)
        for bi, dil in enumerate(DILATIONS):
            runs = res // dil
            q_run = blk // runs
            k_run = 2 * blk // runs
            dist = blk + runs * (a % q_run - c % k_run) + (a // q_run - c // k_run)
            ok = (dist >= 0) & (dist <= blk)
            bias_ref[2 * bi] = jnp.where(ok, 0.0, -jnp.inf)
            bias_ref[2 * bi + 1] = jnp.where(ok & (c % k_run >= q_run), 0.0, -jnp.inf)

    @pl.when(t > 0)
    def _():
        kbuf_ref[:, 0:blk, :] = kbuf_ref[:, blk:2 * blk, :]
        vbuf_ref[:, 0:blk, :] = vbuf_ref[:, blk:2 * blk, :]

    for r in range(res):
        rows = pl.ds(r, blk, stride=res)
        qc_ref[r] = q_ref[rows, :]
        kbuf_ref[r, blk:2 * blk, :] = k_ref[rows, :]
        vbuf_ref[r, blk:2 * blk, :] = v_ref[rows, :]

    ones = jnp.ones((2 * blk, HEAD_DIM), BF16)
    for bi, dil in enumerate(DILATIONS):
        runs = res // dil
        q_run = blk // runs
        k_run = 2 * blk // runs

        def block(idx, bi=bi, dil=dil, runs=runs, q_run=q_run, k_run=k_run):
            r_d = idx // runs
            n = idx % runs
            q0 = pl.multiple_of(n * q_run, q_run)
            k0 = pl.multiple_of(blk + (n - 1) * q_run, q_run)
            planes = [r_d + dil * j for j in range(runs)]
            qb = jnp.concatenate([qc_ref[p, pl.ds(q0, q_run), :] for p in planes], axis=0).astype(BF16)
            kb = jnp.concatenate([kbuf_ref[p, pl.ds(k0, k_run), :] for p in planes], axis=0).astype(BF16)
            vb = jnp.concatenate([vbuf_ref[p, pl.ds(k0, k_run), :] for p in planes], axis=0).astype(BF16)
            first = ((t == 0) & (n == 0)).astype(jnp.int32)
            s = _dot_nt(qb, kb) + bias_ref[2 * bi + first]
            m = jnp.max(s, axis=-1, keepdims=True)
            p = jnp.exp2(s - m).astype(BF16)
            ol = _dot(p, jnp.concatenate([vb, ones], axis=1))
            o = ol[:, :HEAD_DIM]
            l = ol[:, HEAD_DIM:]
            for j, plane in enumerate(planes):
                run = slice(j * q_run, (j + 1) * q_run)
                dst = (plane, pl.ds(q0, q_run), slice(None))
                m_j = jnp.broadcast_to(m[run], (q_run, HEAD_DIM))
                if bi == 0:
                    max_ref[dst] = m_j
                    den_ref[dst] = l[run]
                    num_ref[dst] = o[run]
                else:
                    m_old = max_ref[dst]
                    m_new = jnp.maximum(m_old, m_j)
                    w_old = jnp.exp2(m_old - m_new)
                    w_new = jnp.exp2(m_j - m_new)
                    max_ref[dst] = m_new
                    den_ref[dst] = den_ref[dst] * w_old + l[run] * w_new
                    num_ref[dst] = num_ref[dst] * w_old + o[run] * w_new

        def body(i, carry, block=block):
            for u in range(unroll):
                block(i * unroll + u)
            return carry

        lax.fori_loop(0, res // unroll, body, 0)

    for r in range(res):
        o_ref[pl.ds(r, blk, stride=res), :] = num_ref[r] / den_ref[r]


def _attention(qkv, *, batch, seq, unroll=16):
    tq = ATTN_BLOCK * ATTN_RES
    nt = seq // tq
    d_attn = ATTN_HEADS * HEAD_DIM

    def col(group):
        return pl.BlockSpec((tq, HEAD_DIM), lambda b, h, t: (b * nt + t, group * ATTN_HEADS + h))

    plane = pltpu.VMEM((ATTN_RES, ATTN_BLOCK, HEAD_DIM), F32)
    band = pltpu.VMEM((ATTN_RES, 2 * ATTN_BLOCK, HEAD_DIM), F32)
    return pl.pallas_call(
        functools.partial(_attn_kernel, unroll=unroll),
        grid=(batch, ATTN_HEADS, nt),
        in_specs=[col(0), col(1), col(2)],
        out_specs=pl.BlockSpec((tq, HEAD_DIM), lambda b, h, t: (b * nt + t, h)),
        out_shape=jax.ShapeDtypeStruct((batch * seq, d_attn), F32),
        scratch_shapes=[plane, band, band, plane, plane, plane,
                        pltpu.VMEM((2 * len(DILATIONS), ATTN_BLOCK, 2 * ATTN_BLOCK), F32)],
        compiler_params=_params("parallel", "parallel", "arbitrary"),
        name="dilated_attention",
    )(qkv, qkv, qkv)


def kernel(x, positions, norm_mix, norm_mlp, final_norm, rec_w_in, rec_conv_w, rec_conv_b, lru_w_a, lru_b_a, lru_w_i, lru_b_i, lru_lambda, hgrn_lb_logits, hgrn_g_norm, rec_w_out, attn_w_qkv, attn_w_o, mlp_w1, mlp_w2):
    batch, seq, d = x.shape
    depth = norm_mix.shape[0]
    assert seq % (ATTN_BLOCK * ATTN_RES) == 0 and DILATIONS[-1] == ATTN_RES
    h = x.reshape(batch * seq, d)
    pos = positions.reshape(batch * seq)
    mlp_w1 = mlp_w1.astype(BF16)
    mlp_w2 = mlp_w2.astype(BF16)
    for layer in range(depth):
        j = layer // 2
        if layer % 2 == 0:
            proj, lru = _in_proj_lru(h, norm_mix[layer], rec_w_in[j].astype(BF16), rec_conv_w[j],
                                     rec_conv_b[j], lru_w_a[j].astype(BF16), lru_b_a[j],
                                     lru_w_i[j].astype(BF16), lru_b_i[j], lru_lambda[j], seq=seq)
            hg = _hgrn(proj, hgrn_lb_logits, hgrn_g_norm[j], batch=batch, seq=seq, layer=layer, col0=0)
            h = _out_proj(h, [lru, hg], rec_w_out[j].astype(BF16))
        else:
            qkv = _qkv_proj(h, norm_mix[layer], attn_w_qkv[j].astype(BF16), pos)
            attn = _attention(qkv, batch=batch, seq=seq)
            h = _out_proj(h, [attn], attn_w_o[j].astype(BF16))
        last = layer == depth - 1
        h = _mlp(h, norm_mlp[layer], mlp_w1, mlp_w2, layer, final_norm if last else None)
    if depth == 0:
        h = _rms_norm(h, final_norm)
    return h.reshape(batch, seq, d)
```

```python
import functools
import math

import jax
import jax.numpy as jnp
from jax import lax
from jax.experimental import pallas as pl
from jax.experimental.pallas import tpu as pltpu

F32 = jnp.float32
BF16 = jnp.bfloat16

NORM_EPS = 1e-6
LRU_C = 8.0
LRU_HEADS = 4
CONV_WIDTH = 4
HGRN_HEAD = 128
ATTN_HEADS = 16
HEAD_DIM = 128
ROPE_DIM = 32
ROPE_THETA = 500000.0
DILATIONS = (1, 4, 16)
ATTN_BLOCK = 128
ATTN_RES = 16
LANES = 128
MXU_COLS = 256
NORM_ROW_CHUNK = 256
LRU_ROW_CHUNK = 128
QKV_ROW_CHUNK = 256
QKV_CHUNK_HEADS = 2
CONV_HALO = 8

VMEM_LIMIT = 56 * 1024 * 1024


def _params(*sem):
    return pltpu.CompilerParams(dimension_semantics=sem, vmem_limit_bytes=VMEM_LIMIT)


def _rms_norm(x, gain):
    return x * lax.rsqrt(jnp.mean(x * x, axis=-1, keepdims=True) + NORM_EPS) * gain


def _dot(a, b):
    return jnp.dot(a, b, preferred_element_type=F32)


def _dot_nt(a, b):
    return lax.dot_general(a, b, (((1,), (1,)), ((), ())), preferred_element_type=F32)


def _dot_tn(a, b):
    return lax.dot_general(a, b, (((0,), (0,)), ((), ())), preferred_element_type=F32)


def _qkv_kernel(x_ref, g_ref, w_ref, pos_ref, freq_ref, sign_ref, o_ref, xn_ref, cos_ref, sin_ref,
                *, q_tiles, qk_tiles, heads_per_tile):
    j = pl.program_id(1)
    half = ROPE_DIM // 2

    def rotated_tile(rows, xn, cos, sin):
        lane = lax.broadcasted_iota(jnp.int32, cos.shape, 1)
        for c in range(heads_per_tile // QKV_CHUNK_HEADS):
            c0 = c * QKV_CHUNK_HEADS * HEAD_DIM
            acc = _dot(xn, w_ref[:, c0:c0 + QKV_CHUNK_HEADS * HEAD_DIM])
            for hh in range(QKV_CHUNK_HEADS):
                t = acc[:, hh * HEAD_DIM:(hh + 1) * HEAD_DIM]
                partner = jnp.where(lane < half, pltpu.roll(t, HEAD_DIM - half, axis=1),
                                    pltpu.roll(t, half, axis=1))
                o_ref[rows, c0 + hh * HEAD_DIM:c0 + (hh + 1) * HEAD_DIM] = t * cos + partner * sin

    @pl.when(j == 0)
    def _():
        for r in range(0, x_ref.shape[0], NORM_ROW_CHUNK):
            rows = pl.ds(r, NORM_ROW_CHUNK)
            xn = _rms_norm(x_ref[rows, :], g_ref[...]).astype(BF16)
            xn_ref[rows, :] = xn
            ang = pos_ref[rows, :] * freq_ref[...]
            cos = jnp.cos(ang)
            sin = jnp.sin(ang) * sign_ref[...]
            scale = F32(HEAD_DIM ** -0.5 * math.log2(math.e))
            cos_ref[0, rows, :] = cos * scale
            sin_ref[0, rows, :] = sin * scale
            cos_ref[1, rows, :] = cos
            sin_ref[1, rows, :] = sin
            rotated_tile(rows, xn, cos * scale, sin * scale)

    @pl.when((j > 0) & (j < qk_tiles))
    def _():
        group = (j >= q_tiles).astype(jnp.int32)
        for r in range(0, x_ref.shape[0], QKV_ROW_CHUNK):
            rows = pl.ds(r, QKV_ROW_CHUNK)
            rotated_tile(rows, xn_ref[rows, :], cos_ref[group, rows, :], sin_ref[group, rows, :])

    @pl.when(j >= qk_tiles)
    def _():
        o_ref[...] = _dot(xn_ref[...], w_ref[...])


def _qkv_proj(x, gain, w, pos, *, tm=1024, tn=1024):
    m, d = x.shape
    n = w.shape[1]
    half = ROPE_DIM // 2
    inv_freq = 1.0 / (ROPE_THETA ** (jnp.arange(half, dtype=F32) * (2.0 / ROPE_DIM)))
    zeros = jnp.zeros((HEAD_DIM - ROPE_DIM,), F32)
    freq = jnp.concatenate([inv_freq, inv_freq, zeros]).reshape(1, HEAD_DIM)
    sign = jnp.concatenate([-jnp.ones((half,), F32), jnp.ones((half,), F32), zeros]).reshape(1, HEAD_DIM)
    d_attn = n // 3
    kern = functools.partial(_qkv_kernel, q_tiles=d_attn // tn, qk_tiles=2 * d_attn // tn,
                             heads_per_tile=tn // HEAD_DIM)
    return pl.pallas_call(
        kern,
        grid=(m // tm, n // tn),
        in_specs=[pl.BlockSpec((tm, d), lambda i, j: (i, 0)),
                  pl.BlockSpec((1, d), lambda i, j: (0, 0)),
                  pl.BlockSpec((d, tn), lambda i, j: (0, j)),
                  pl.BlockSpec((tm, 1), lambda i, j: (i, 0)),
                  pl.BlockSpec((1, HEAD_DIM), lambda i, j: (0, 0)),
                  pl.BlockSpec((1, HEAD_DIM), lambda i, j: (0, 0))],
        out_specs=pl.BlockSpec((tm, tn), lambda i, j: (i, j)),
        out_shape=jax.ShapeDtypeStruct((m, n), F32),
        scratch_shapes=[pltpu.VMEM((tm, d), BF16), pltpu.VMEM((2, tm, HEAD_DIM), F32),
                        pltpu.VMEM((2, tm, HEAD_DIM), F32)],
        compiler_params=_params("parallel", "arbitrary"),
        name="qkv_rope",
    )(x, gain.reshape(1, d), w, pos.astype(F32).reshape(m, 1), freq, sign)


def _out_proj_kernel(*refs, n_parts):
    h_ref = refs[0]
    a_refs = refs[1:1 + n_parts]
    w_refs = refs[1 + n_parts:1 + 2 * n_parts]
    o_ref = refs[1 + 2 * n_parts]
    acc = h_ref[...]
    for a_ref, w_ref in zip(a_refs, w_refs):
        acc = acc + _dot(a_ref[...].astype(BF16), w_ref[...])
    o_ref[...] = acc


def _out_proj(h, parts, w, *, tm=512):
    m, n = h.shape
    k_part = parts[0].shape[1]
    assert all(a.shape[1] == k_part for a in parts) and k_part * len(parts) == w.shape[0]
    row = pl.BlockSpec((tm, n), lambda i: (i, 0))
    in_specs = [row]
    in_specs += [pl.BlockSpec((tm, k_part), lambda i: (i, 0)) for _ in parts]
    in_specs += [pl.BlockSpec((k_part, n), lambda i, p=p: (p, 0)) for p in range(len(parts))]
    return pl.pallas_call(
        functools.partial(_out_proj_kernel, n_parts=len(parts)),
        grid=(m // tm,),
        in_specs=in_specs,
        out_specs=row,
        out_shape=jax.ShapeDtypeStruct((m, n), F32),
        compiler_params=_params("parallel"),
        name="out_proj",
    )(h, *parts, *([w] * len(parts)))


def _mlp_kernel(h_ref, g_ref, w1_ref, w2_ref, fg_ref, o_ref, xn_ref, *, final_norm):
    k = pl.program_id(1)
    last = pl.num_programs(1) - 1
    row_chunks = [pl.ds(r, NORM_ROW_CHUNK) for r in range(0, h_ref.shape[0], NORM_ROW_CHUNK)]

    def contribution(xn):
        a = jnp.maximum(_dot(xn, w1_ref[...]), 0.0)
        return _dot((a * a).astype(BF16), w2_ref[...])

    @pl.when(k == 0)
    def _():
        for rows in row_chunks:
            x = h_ref[rows, :]
            xn = _rms_norm(x, g_ref[...]).astype(BF16)
            xn_ref[rows, :] = xn
            o_ref[rows, :] = x + contribution(xn)

    @pl.when((k > 0) & (k < last) if final_norm else k > 0)
    def _():
        o_ref[...] += contribution(xn_ref[...])

    if final_norm:
        @pl.when(k == last)
        def _():
            for rows in row_chunks:
                acc = o_ref[rows, :] + contribution(xn_ref[rows, :])
                o_ref[rows, :] = _rms_norm(acc, fg_ref[...])


def _mlp(h, gain, w1, w2, layer, final_gain=None, *, tm=1024, tf=512):
    m, d = h.shape
    f = w1.shape[2]
    fg = jnp.ones((d,), F32) if final_gain is None else final_gain
    return pl.pallas_call(
        functools.partial(_mlp_kernel, final_norm=final_gain is not None),
        grid=(m // tm, f // tf),
        in_specs=[pl.BlockSpec((tm, d), lambda i, k: (i, 0)),
                  pl.BlockSpec((1, d), lambda i, k: (0, 0)),
                  pl.BlockSpec((None, d, tf), lambda i, k: (layer, 0, k)),
                  pl.BlockSpec((None, tf, d), lambda i, k: (layer, k, 0)),
                  pl.BlockSpec((1, d), lambda i, k: (0, 0))],
        out_specs=pl.BlockSpec((tm, d), lambda i, k: (i, 0)),
        out_shape=jax.ShapeDtypeStruct((m, d), F32),
        scratch_shapes=[pltpu.VMEM((tm, d), BF16)],
        compiler_params=_params("parallel", "arbitrary"),
        name="mlp",
    )(h, gain.reshape(1, d), w1, w2, fg.reshape(1, d))


SUBLANES = 8


def _linear_scan(a, u, carry):
    rows, width = a.shape
    groups = rows // SUBLANES
    a = a.reshape(groups, SUBLANES, width)
    u = u.reshape(groups, SUBLANES, width)
    sub = lax.broadcasted_iota(jnp.int32, a.shape, 1)
    d = 1
    while d < SUBLANES:
        keep = sub >= d
        u = u + a * jnp.where(keep, pltpu.roll(u, d, axis=1), 0.0)
        a = a * jnp.where(keep, pltpu.roll(a, d, axis=1), 1.0)
        d *= 2
    out = []
    for g in range(groups):
        h = u[g] + a[g] * carry
        carry = h[SUBLANES - 1:SUBLANES]
        out.append(h)
    return jnp.concatenate(out, axis=0), carry


def _lru_rows(hh, r0, n_rows, xbuf_ref, ybuf_ref, cw_ref, cb_ref, wa_ref, ba_ref, wi_ref, bi_ref, lam_ref,
              carry):
    blk = ybuf_ref.shape[1] // LRU_HEADS
    cols = slice(hh * blk, (hh + 1) * blk)
    tap0 = CONV_HALO - CONV_WIDTH + 1 + r0
    xc = cb_ref[:, cols] + cw_ref[0:1, cols] * xbuf_ref[pl.ds(tap0, n_rows), cols]
    for j in range(1, CONV_WIDTH):
        xc = xc + cw_ref[j:j + 1, cols] * xbuf_ref[pl.ds(tap0 + j, n_rows), cols]

    xh = xc.astype(BF16)
    r = jax.nn.sigmoid(_dot(xh, wa_ref[hh]) + ba_ref[:, cols])
    gi = jax.nn.sigmoid(_dot(xh, wi_ref[hh]) + bi_ref[:, cols])

    neg_lam = -lam_ref[:, cols]
    softplus = jnp.maximum(neg_lam, 0.0) + jnp.log1p(jnp.exp(-jnp.abs(neg_lam)))
    log_a = (-LRU_C) * r * softplus
    a = jnp.exp(log_a)
    u = jnp.sqrt(-jnp.tanh(log_a) * (a * a + 1.0)) * (gi * xc)

    h, carry = _linear_scan(a, u, carry)

    y = ybuf_ref[pl.ds(r0, n_rows), cols]
    gelu = 0.5 * y * (1.0 + jnp.tanh(math.sqrt(2.0 / math.pi) * (y + 0.044715 * (y * y * y))))
    return (h * gelu).astype(BF16), carry


def _in_proj_lru_kernel(x_ref, g_ref, w_ref, cw_ref, cb_ref, wa_ref, ba_ref, wi_ref, bi_ref, lam_ref,
                        proj_ref, lru_ref, xn_ref, xbuf_ref, ybuf_ref, carry_ref, *, tiles_per_seq):
    i = pl.program_id(0)
    j = pl.program_id(1)
    tm = x_ref.shape[0]

    @pl.when(j == 0)
    def _():
        @pl.when(i % tiles_per_seq == 0)
        def _():
            xbuf_ref[0:CONV_HALO, :] = jnp.zeros((CONV_HALO, xbuf_ref.shape[1]), F32)
            carry_ref[...] = jnp.zeros(carry_ref.shape, F32)

        @pl.when(i % tiles_per_seq != 0)
        def _():
            xbuf_ref[0:CONV_HALO, :] = xbuf_ref[tm:tm + CONV_HALO, :]

        for r in range(0, tm, NORM_ROW_CHUNK):
            rows = pl.ds(r, NORM_ROW_CHUNK)
            xn = _rms_norm(x_ref[rows, :], g_ref[...]).astype(BF16)
            xn_ref[rows, :] = xn
            xbuf_ref[pl.ds(CONV_HALO + r, NORM_ROW_CHUNK), :] = _dot(xn, w_ref[...])

    @pl.when(j == 1)
    def _():
        ybuf_ref[...] = _dot(xn_ref[...], w_ref[...])

    for hh in range(LRU_HEADS):
        @pl.when(j == 2 + hh)
        def _(hh=hh):
            blk = ybuf_ref.shape[1] // LRU_HEADS
            cols = slice(hh * blk, (hh + 1) * blk)
            carry = carry_ref[:, cols]
            n_pieces = tm // LRU_ROW_CHUNK
            n_col_chunks = w_ref.shape[1] // MXU_COLS
            row_parts = n_pieces // n_col_chunks
            part_rows = tm // row_parts
            for piece in range(n_pieces):
                c_cols = slice((piece // row_parts) * MXU_COLS, (piece // row_parts + 1) * MXU_COLS)
                p_rows = pl.ds((piece % row_parts) * part_rows, part_rows)
                proj_ref[p_rows, c_cols] = _dot(xn_ref[p_rows, :], w_ref[:, c_cols])
                r0 = piece * LRU_ROW_CHUNK
                out, carry = _lru_rows(hh, r0, LRU_ROW_CHUNK, xbuf_ref, ybuf_ref, cw_ref, cb_ref,
                                       wa_ref, ba_ref, wi_ref, bi_ref, lam_ref, carry)
                lru_ref[pl.ds(r0, LRU_ROW_CHUNK), :] = out
            carry_ref[:, cols] = carry


def _in_proj_lru(x, gain, w, conv_w, conv_b, w_a, b_a, w_i, b_i, lam, *, seq, tm=1024):
    m, d = x.shape
    width = conv_w.shape[1]
    tn = width
    n_steps = w.shape[1] // tn
    assert n_steps == 2 + LRU_HEADS and seq % tm == 0
    blk = width // LRU_HEADS
    const2 = lambda i, j: (0, 0)
    vec = pl.BlockSpec((1, width), const2)
    gate_w = pl.BlockSpec(w_a.shape, lambda i, j: (0, 0, 0))
    late = lambda i, j: (i, jnp.maximum(j - 2, 0))
    return pl.pallas_call(
        functools.partial(_in_proj_lru_kernel, tiles_per_seq=seq // tm),
        grid=(m // tm, n_steps),
        in_specs=[pl.BlockSpec((tm, d), lambda i, j: (i, 0)),
                  pl.BlockSpec((1, d), const2),
                  pl.BlockSpec((d, tn), lambda i, j: (0, j)),
                  pl.BlockSpec((CONV_WIDTH, width), const2), vec,
                  gate_w, vec, gate_w, vec, vec],
        out_specs=[pl.BlockSpec((tm, tn), late), pl.BlockSpec((tm, blk), late)],
        out_shape=[jax.ShapeDtypeStruct((m, w.shape[1] - 2 * width), F32),
                   jax.ShapeDtypeStruct((m, width), BF16)],
        scratch_shapes=[pltpu.VMEM((tm, d), BF16), pltpu.VMEM((tm + CONV_HALO, width), F32),
                        pltpu.VMEM((tm, width), F32), pltpu.VMEM((1, width), F32)],
        compiler_params=_params("arbitrary", "arbitrary"),
        name="in_proj_rg_lru",
    )(x, gain.reshape(1, d), w, conv_w, conv_b.reshape(1, width), w_a, b_a.reshape(1, width),
      w_i, b_i.reshape(1, width), lam.reshape(1, width))


def _cumsum_rows(x):
    rows = x.shape[0]
    row = lax.broadcasted_iota(jnp.int32, x.shape, 0)
    d = 1
    while d < rows:
        x = x + jnp.where(row >= d, pltpu.roll(x, d, axis=0), 0.0)
        d *= 2
    return x


def _shift_rows(x, s):
    rows, width = x.shape
    if abs(s) < SUBLANES:
        x3 = x.reshape(rows // SUBLANES, SUBLANES, width)
        return pltpu.roll(x3, s % SUBLANES, axis=1).reshape(rows, width)
    return pltpu.roll(x, s % rows, axis=0)


def _hgrn_kernel(q_ref, f_ref, v_ref, g_ref, lbl_ref, gn_ref, o_ref, state_ref, level_ref,
                 *, layer, chunk, n_chunks):
    ti = lax.broadcasted_iota(jnp.int32, (chunk, chunk), 0)
    si = lax.broadcasted_iota(jnp.int32, (chunk, chunk), 1)
    tx = ti ^ si

    @pl.when(pl.program_id(2) == 0)
    def _():
        state_ref[...] = jnp.zeros(state_ref.shape, F32)
        for li in range(level_ref.shape[0]):
            s = 1 << li
            level_ref[li] = jnp.where((tx >= s) & (tx < 2 * s) & (ti > si), 1.0, 0.0)

    logits = lbl_ref[...]
    e = jnp.exp(logits - jnp.max(logits, axis=0, keepdims=True))
    lb = jnp.sum(e[0:layer + 1, :], axis=0, keepdims=True) / jnp.sum(e, axis=0, keepdims=True)

    row = lax.broadcasted_iota(jnp.int32, (chunk, HGRN_HEAD), 0)

    for c in range(n_chunks):
        rows = pl.ds(c * chunk, chunk)
        qr = q_ref[rows, :]
        fz = f_ref[rows, :]
        v = v_ref[rows, :].astype(BF16)
        q = qr * jax.nn.sigmoid(qr)
        log_f = jnp.log(lb + (1.0 - lb) * jax.nn.sigmoid(fz))
        kk = (1.0 - lb) * jax.nn.sigmoid(-fz)
        b = _cumsum_rows(log_f)

        scores = jnp.where(tx == 0, jnp.sum(q * kk, axis=-1, keepdims=True), 0.0)
        b_end = b
        s = 1
        for li in range(level_ref.shape[0]):
            upper = (row & s) != 0
            decay = jnp.exp(jnp.where(upper, b - _shift_rows(b_end, s), b_end - b))
            z = (jnp.where(upper, q, kk) * decay).astype(BF16)
            scores = scores + _dot_nt(z, z) * level_ref[li]
            b_end = jnp.where(upper, b_end, _shift_rows(b_end, -s))
            s *= 2

        state = state_ref[...]
        o = _dot(scores.astype(BF16), v) + _dot_nt((q * jnp.exp(b)).astype(BF16), state.astype(BF16))
        kd = (kk * jnp.exp(b_end - b)).astype(BF16)
        state_ref[...] = state * jnp.exp(b_end[0:1, :]) + _dot_tn(v, kd)

        o = o * lax.rsqrt(jnp.mean(o * o, axis=-1, keepdims=True) + NORM_EPS) * gn_ref[...]
        gr = g_ref[rows, :]
        o_ref[rows, :] = (o * (gr * jax.nn.sigmoid(gr))).astype(o_ref.dtype)


def _hgrn(proj, lb_logits, g_norm, *, batch, seq, layer, col0, tt=2048, chunk=128):
    width = g_norm.shape[0]
    heads = width // HGRN_HEAD
    nt = seq // tt
    hb = width // HGRN_HEAD
    c0 = col0 // HGRN_HEAD

    def col(group):
        return pl.BlockSpec((tt, HGRN_HEAD), lambda b, h, t: (b * nt + t, c0 + group * hb + h))

    return pl.pallas_call(
        functools.partial(_hgrn_kernel, layer=layer, chunk=chunk, n_chunks=tt // chunk),
        grid=(batch, heads, nt),
        in_specs=[col(0), col(1), col(2), col(3),
                  pl.BlockSpec((lb_logits.shape[0], HGRN_HEAD), lambda b, h, t: (0, h)),
                  pl.BlockSpec((1, HGRN_HEAD), lambda b, h, t: (0, h))],
        out_specs=pl.BlockSpec((tt, HGRN_HEAD), lambda b, h, t: (b * nt + t, h)),
        out_shape=jax.ShapeDtypeStruct((batch * seq, width), BF16),
        scratch_shapes=[pltpu.VMEM((HGRN_HEAD, HGRN_HEAD), F32),
                        pltpu.VMEM((chunk.bit_length() - 1, chunk, chunk), F32)],
        compiler_params=_params("parallel", "parallel", "arbitrary"),
        name="hgrn2",
    )(proj, proj, proj, proj, lb_logits, g_norm.reshape(1, width))


def _attn_kernel(q_ref, k_ref, v_ref, o_ref, qc_ref, kbuf_ref, vbuf_ref, num_ref, max_ref, den_ref, bias_ref,
                 *, unroll):
    t = pl.program_id(2)
    blk = ATTN_BLOCK
    res = ATTN_RES

    @pl.when(t == 0)
    def _():
        kbuf_ref[:, 0:blk, :] = jnp.zeros((res, blk, HEAD_DIM), F32)
        vbuf_ref[:, 0:blk, :] = jnp.zeros((res, blk, HEAD_DIM), F32)
        a = lax.broadcasted_iota(jnp.int32, (blk, 2 * blk), 0)
        c = lax.broadcasted_iota(jnp.int32, (blk, 2 * blk), 1)
        for bi, dil in enumerate(DILATIONS):
            runs = res // dil
            q_run = blk // runs
            k_run = 2 * blk // runs
            dist = blk + runs * (a % q_run - c % k_run) + (a // q_run - c // k_run)
            ok = (dist >= 0) & (dist <= blk)
            bias_ref[2 * bi] = jnp.where(ok, 0.0, -jnp.inf)
            bias_ref[2 * bi + 1] = jnp.where(ok & (c % k_run >= q_run), 0.0, -jnp.inf)

    @pl.when(t > 0)
    def _():
        kbuf_ref[:, 0:blk, :] = kbuf_ref[:, blk:2 * blk, :]
        vbuf_ref[:, 0:blk, :] = vbuf_ref[:, blk:2 * blk, :]

    for r in range(res):
        rows = pl.ds(r, blk, stride=res)
        qc_ref[r] = q_ref[rows, :]
        kbuf_ref[r, blk:2 * blk, :] = k_ref[rows, :]
        vbuf_ref[r, blk:2 * blk, :] = v_ref[rows, :]

    ones = jnp.ones((2 * blk, HEAD_DIM), BF16)
    for bi, dil in enumerate(DILATIONS):
        runs = res // dil
        q_run = blk // runs
        k_run = 2 * blk // runs

        def block(idx, bi=bi, dil=dil, runs=runs, q_run=q_run, k_run=k_run):
            r_d = idx // runs
            n = idx % runs
            q0 = pl.multiple_of(n * q_run, q_run)
            k0 = pl.multiple_of(blk + (n - 1) * q_run, q_run)
            planes = [r_d + dil * j for j in range(runs)]
            qb = jnp.concatenate([qc_ref[p, pl.ds(q0, q_run), :] for p in planes], axis=0).astype(BF16)
            kb = jnp.concatenate([kbuf_ref[p, pl.ds(k0, k_run), :] for p in planes], axis=0).astype(BF16)
            vb = jnp.concatenate([vbuf_ref[p, pl.ds(k0, k_run), :] for p in planes], axis=0).astype(BF16)
            first = ((t == 0) & (n == 0)).astype(jnp.int32)
            s = _dot_nt(qb, kb) + bias_ref[2 * bi + first]
            m = jnp.max(s, axis=-1, keepdims=True)
            p = jnp.exp2(s - m).astype(BF16)
            ol = _dot(p, jnp.concatenate([vb, ones], axis=1))
            o = ol[:, :HEAD_DIM]
            l = ol[:, HEAD_DIM:]
            for j, plane in enumerate(planes):
                run = slice(j * q_run, (j + 1) * q_run)
                dst = (plane, pl.ds(q0, q_run), slice(None))
                m_j = jnp.broadcast_to(m[run], (q_run, HEAD_DIM))
                if bi == 0:
                    max_ref[dst] = m_j
                    den_ref[dst] = l[run]
                    num_ref[dst] = o[run]
                else:
                    m_old = max_ref[dst]
                    m_new = jnp.maximum(m_old, m_j)
                    w_old = jnp.exp2(m_old - m_new)
                    w_new = jnp.exp2(m_j - m_new)
                    max_ref[dst] = m_new
                    den_ref[dst] = den_ref[dst] * w_old + l[run] * w_new
                    num_ref[dst] = num_ref[dst] * w_old + o[run] * w_new

        def body(i, carry, block=block):
            for u in range(unroll):
                block(i * unroll + u)
            return carry

        lax.fori_loop(0, res // unroll, body, 0)

    for r in range(res):
        o_ref[pl.ds(r, blk, stride=res), :] = num_ref[r] / den_ref[r]


def _attention(qkv, *, batch, seq, unroll=16):
    tq = ATTN_BLOCK * ATTN_RES
    nt = seq // tq
    d_attn = ATTN_HEADS * HEAD_DIM

    def col(group):
        return pl.BlockSpec((tq, HEAD_DIM), lambda b, h, t: (b * nt + t, group * ATTN_HEADS + h))

    plane = pltpu.VMEM((ATTN_RES, ATTN_BLOCK, HEAD_DIM), F32)
    band = pltpu.VMEM((ATTN_RES, 2 * ATTN_BLOCK, HEAD_DIM), F32)
    return pl.pallas_call(
        functools.partial(_attn_kernel, unroll=unroll),
        grid=(batch, ATTN_HEADS, nt),
        in_specs=[col(0), col(1), col(2)],
        out_specs=pl.BlockSpec((tq, HEAD_DIM), lambda b, h, t: (b * nt + t, h)),
        out_shape=jax.ShapeDtypeStruct((batch * seq, d_attn), F32),
        scratch_shapes=[plane, band, band, plane, plane, plane,
                        pltpu.VMEM((2 * len(DILATIONS), ATTN_BLOCK, 2 * ATTN_BLOCK), F32)],
        compiler_params=_params("parallel", "parallel", "arbitrary"),
        name="dilated_attention",
    )(qkv, qkv, qkv)


def kernel(x, positions, norm_mix, norm_mlp, final_norm, rec_w_in, rec_conv_w, rec_conv_b, lru_w_a, lru_b_a, lru_w_i, lru_b_i, lru_lambda, hgrn_lb_logits, hgrn_g_norm, rec_w_out, attn_w_qkv, attn_w_o, mlp_w1, mlp_w2):
    batch, seq, d = x.shape
    depth = norm_mix.shape[0]
    assert seq % (ATTN_BLOCK * ATTN_RES) == 0 and DILATIONS[-1] == ATTN_RES
    h = x.reshape(batch * seq, d)
    pos = positions.reshape(batch * seq)
    mlp_w1 = mlp_w1.astype(BF16)
    mlp_w2 = mlp_w2.astype(BF16)
    for layer in range(depth):
        j = layer // 2
        if layer % 2 == 0:
            proj, lru = _in_proj_lru(h, norm_mix[layer], rec_w_in[j].astype(BF16), rec_conv_w[j],
                                     rec_conv_b[j], lru_w_a[j].astype(BF16), lru_b_a[j],
                                     lru_w_i[j].astype(BF16), lru_b_i[j], lru_lambda[j], seq=seq)
            hg = _hgrn(proj, hgrn_lb_logits, hgrn_g_norm[j], batch=batch, seq=seq, layer=layer, col0=0)
            h = _out_proj(h, [lru, hg], rec_w_out[j].astype(BF16))
        else:
            qkv = _qkv_proj(h, norm_mix[layer], attn_w_qkv[j].astype(BF16), pos)
            attn = _attention(qkv, batch=batch, seq=seq)
            h = _out_proj(h, [attn], attn_w_o[j].astype(BF16))
        last = layer == depth - 1
        h = _mlp(h, norm_mlp[layer], mlp_w1, mlp_w2, layer, final_norm if last else None)
    if depth == 0:
        h = _rms_norm(h, final_norm)
    return h.reshape(batch, seq, d)
```
